```python
import math
import jax, jax.numpy as jnp
from jax import lax
import numpy as np

D_MODEL = 2048
BATCH = 4
SEQ = 4096
DEPTH = 1

HEAD_DIM = 128
ATTN_HEADS = 8
ATTN_KV_HEADS = 2
ATTN_GROUP = ATTN_HEADS // ATTN_KV_HEADS
WINDOW = 128
ATTN_BLOCK = 128
N_BUCKETS = 32
MAX_DISTANCE = 128
DN_HEADS = 8
DN_KEY_DIM = 128
DN_VAL_DIM = 128
CONV_WIDTH = 5
DN_CHUNK = 64
MIX_WIDTH = ATTN_HEADS * HEAD_DIM + DN_HEADS * DN_VAL_DIM
PEER_HEADS = 8
PEER_NKEYS = 128
PEER_EXPERTS = PEER_NKEYS * PEER_NKEYS
PEER_QDIM = 256
PEER_HALF = PEER_QDIM // 2
PEER_TOPK = 16
PEER_BLOCK = 128
DEEPNORM_ALPHA = (2.0 * DEPTH) ** 0.25
DEEPNORM_BETA = (8.0 * DEPTH) ** -0.25
LN_EPS = 1e-5
RMS_EPS = 1e-6

SPLIT_SIZES = (ATTN_HEADS * HEAD_DIM, ATTN_KV_HEADS * HEAD_DIM, ATTN_KV_HEADS * HEAD_DIM,
               DN_HEADS * DN_KEY_DIM, DN_HEADS * DN_KEY_DIM, DN_HEADS * DN_VAL_DIM, DN_HEADS * DN_VAL_DIM,
               2 * DN_HEADS, 2 * DN_HEADS)
IN_WIDTH = int(sum(SPLIT_SIZES))
SPLIT_POINTS = tuple(int(c) for c in np.cumsum(SPLIT_SIZES)[:-1])
CONV_CH = 2 * DN_HEADS * DN_KEY_DIM + DN_HEADS * DN_VAL_DIM

kernel_name = "hymba_swa_gdn_peer_deepnorm"


def _layer_norm(x, g, b):
    xf = x.astype(jnp.float32)
    mu = jnp.mean(xf, axis=-1, keepdims=True)
    var = jnp.mean(jnp.square(xf - mu), axis=-1, keepdims=True)
    return ((xf - mu) * lax.rsqrt(var + LN_EPS) * g.astype(jnp.float32) + b.astype(jnp.float32)).astype(x.dtype)


def _l2norm(t):
    return t * lax.rsqrt(jnp.sum(t * t, axis=-1, keepdims=True) + RMS_EPS)


def _t5_bucket(rel):
    nb = N_BUCKETS // 2
    max_exact = nb // 2
    ret = jnp.where(rel > 0, nb, 0)
    n = jnp.abs(rel)
    large = max_exact + (jnp.log(jnp.maximum(n, 1).astype(jnp.float32) / max_exact)
                         / math.log(MAX_DISTANCE / max_exact) * (nb - max_exact)).astype(jnp.int32)
    large = jnp.minimum(large, nb - 1)
    return ret + jnp.where(n < max_exact, n, large)


def _window_attention(q, k, v, sink, rel_bias):
    B, S = q.shape[0], q.shape[1]
    nb = S // ATTN_BLOCK
    qb = q.reshape(B, nb, ATTN_BLOCK, ATTN_KV_HEADS, ATTN_GROUP, HEAD_DIM)

    def band(t):
        tp = jnp.pad(t, ((0, 0), (ATTN_BLOCK, ATTN_BLOCK), (0, 0), (0, 0)))
        parts = [tp[:, o * ATTN_BLOCK: o * ATTN_BLOCK + S].reshape(B, nb, ATTN_BLOCK, ATTN_KV_HEADS, HEAD_DIM)
                 for o in range(3)]
        return jnp.concatenate(parts, axis=2)

    kb, vb = band(k), band(v)
    rel = jnp.arange(3 * ATTN_BLOCK)[None, :] - jnp.arange(ATTN_BLOCK)[:, None] - ATTN_BLOCK
    bias = rel_bias[_t5_bucket(rel)]
    bias = jnp.transpose(bias, (2, 0, 1)).reshape(ATTN_KV_HEADS, ATTN_GROUP, ATTN_BLOCK, 3 * ATTN_BLOCK)
    kpos = (jnp.arange(nb)[:, None] - 1) * ATTN_BLOCK + jnp.arange(3 * ATTN_BLOCK)[None, :]
    mask = (jnp.abs(rel) <= WINDOW)[None] & ((kpos >= 0) & (kpos < S))[:, None, :]
    s = (jnp.einsum('bnqhgd,bnkhd->bnhgqk', qb, kb).astype(jnp.float32) * (HEAD_DIM ** -0.5)
         + bias.astype(jnp.float32))
    s = jnp.where(mask[None, :, None, None], s, -jnp.inf)
    sk = sink.astype(jnp.float32).reshape(1, 1, ATTN_KV_HEADS, ATTN_GROUP, 1, 1)
    m = jnp.maximum(jnp.max(s, axis=-1, keepdims=True), sk)
    p = jnp.exp(s - m)
    den = jnp.sum(p, axis=-1, keepdims=True) + jnp.exp(sk - m)
    o = jnp.einsum('bnhgqk,bnkhd->bnqhgd', (p / den).astype(v.dtype), vb)
    return o.reshape(B, S, ATTN_HEADS * HEAD_DIM)


def _short_conv(x, w):
    pad = CONV_WIDTH // 2
    y = lax.conv_general_dilated(x, w[:, None, :].astype(x.dtype), window_strides=(1,),
                                 padding=((pad, pad),), dimension_numbers=('NWC', 'WIO', 'NWC'),
                                 feature_group_count=x.shape[-1])
    return jax.nn.silu(y)


def _chunk_gated_delta(q, k, v, g, beta):
    B, H, S, Dk = q.shape
    Dv = v.shape[-1]
    C = DN_CHUNK
    N = S // C
    chunks = lambda t: t.reshape((B, H, N, C) + t.shape[3:])
    q = chunks(q * (Dk ** -0.5))
    k = chunks(k)
    v = chunks(v)
    beta = chunks(beta)
    g = jnp.cumsum(chunks(g), axis=-1)
    incl = jnp.tril(jnp.ones((C, C), bool))
    strict = jnp.tril(jnp.ones((C, C), bool), -1)
    diff = g[..., :, None] - g[..., None, :]
    decay = jnp.where(incl, jnp.exp(jnp.where(incl, diff, 0.0)), 0.0)
    kb = k * beta[..., None]
    L = jnp.where(strict, jnp.einsum('bhnid,bhnjd->bhnij', kb, k) * decay, 0.0)
    eye = jnp.eye(C, dtype=L.dtype)
    T = lax.linalg.triangular_solve(L + eye, jnp.broadcast_to(eye, L.shape), left_side=True,
                                    lower=True, unit_diagonal=True)
    u = T @ (v * beta[..., None])
    w = T @ (kb * jnp.exp(g)[..., None])
    A = jnp.einsum('bhnid,bhnjd->bhnij', q, k) * decay
    qg = q * jnp.exp(g)[..., None]
    kd = k * jnp.exp(g[..., -1:] - g)[..., None]
    gl = jnp.exp(g[..., -1])

    def step(state, inp):
        qg_i, kd_i, u_i, w_i, A_i, gl_i = inp
        v_new = u_i - w_i @ state
        o = qg_i @ state + A_i @ v_new
        state = state * gl_i[..., None, None] + jnp.einsum('bhck,bhcv->bhkv', kd_i, v_new)
        return state, o

    xs = tuple(jnp.moveaxis(t, 2, 0) for t in (qg, kd, u, w, A, gl))
    _, o = lax.scan(step, jnp.zeros((B, H, Dk, Dv), q.dtype), xs)
    return jnp.moveaxis(o, 0, 2).reshape(B, H, S, Dv)


def _deltanet(dq, dk, dv, dz, dbeta, da, conv_w, a_log, dt_bias, norm_w):
    B, S = dq.shape[0], dq.shape[1]
    f32 = jnp.float32
    qkv = _short_conv(jnp.concatenate([dq, dk, dv], axis=-1), conv_w)
    q, k, v = jnp.split(qkv, [DN_HEADS * DN_KEY_DIM, 2 * DN_HEADS * DN_KEY_DIM], axis=-1)
    heads = lambda t, d: t.reshape(B, S, DN_HEADS, d).transpose(0, 2, 1, 3).astype(f32)
    q = _l2norm(heads(q, DN_KEY_DIM))
    k = _l2norm(heads(k, DN_KEY_DIM))
    v = heads(v, DN_VAL_DIM)
    beta = jax.nn.sigmoid(dbeta.astype(f32)).reshape(B, S, 2, DN_HEADS).transpose(2, 0, 3, 1)
    a = da.astype(f32).reshape(B, S, 2, DN_HEADS).transpose(2, 0, 3, 1)
    g = -jnp.exp(a_log.astype(f32))[:, None, :, None] * jax.nn.softplus(a + dt_bias.astype(f32)[:, None, :, None])
    o_fwd = _chunk_gated_delta(q, k, v, g[0], beta[0])
    flip = lambda t: jnp.flip(t, axis=2)
    o_bwd = flip(_chunk_gated_delta(flip(q), flip(k), flip(v), flip(g[1]), flip(beta[1])))
    o = (o_fwd + o_bwd).transpose(0, 2, 1, 3)
    o = o * lax.rsqrt(jnp.mean(o * o, axis=-1, keepdims=True) + RMS_EPS) * norm_w.astype(f32)
    z = dz.reshape(B, S, DN_HEADS, DN_VAL_DIM).astype(f32)
    return (o * jax.nn.silu(z)).reshape(B, S, DN_HEADS * DN_VAL_DIM).astype(dq.dtype)


def _peer(h, wq, keys, u, v):
    B, S, D = h.shape
    T = B * S
    t = h.reshape(T, D)
    q = (t @ wq).reshape(T, PEER_HEADS, 2, PEER_HALF)
    s = jnp.einsum('thpd,pkd->thpk', q, keys).astype(jnp.float32)
    sv, si = lax.top_k(s, PEER_TOPK)
    cand = (sv[:, :, 0, :, None] + sv[:, :, 1, None, :]).reshape(T, PEER_HEADS, PEER_TOPK * PEER_TOPK)
    cidx = (si[:, :, 0, :, None] * PEER_NKEYS + si[:, :, 1, None, :]).reshape(T, PEER_HEADS, PEER_TOPK * PEER_TOPK)
    top_s, top_c = lax.top_k(cand, PEER_TOPK)
    eidx = jnp.take_along_axis(cidx, top_c, axis=-1).reshape(T, PEER_HEADS * PEER_TOPK)
    gate = jax.nn.softmax(top_s, axis=-1).reshape(T, PEER_HEADS * PEER_TOPK).astype(h.dtype)
    nblk = T // PEER_BLOCK

    def block(args):
        tb, ib, gb = args
        act = jax.nn.gelu(jnp.einsum('tpd,td->tp', u[ib], tb), approximate=False)
        return jnp.einsum('tp,tpd->td', act * gb, v[ib])

    out = lax.map(block, (t.reshape(nblk, PEER_BLOCK, D),
                          eidx.reshape(nblk, PEER_BLOCK, PEER_HEADS * PEER_TOPK),
                          gate.reshape(nblk, PEER_BLOCK, PEER_HEADS * PEER_TOPK)))
    return out.reshape(B, S, D)


def _layer(x, w_in, conv_w, a_log, dt_bias, dn_norm_w, attn_sink, rel_bias, w_out,
           ln1_g, ln1_b, peer_wq, peer_keys, peer_u, peer_v, ln2_g, ln2_b):
    B, S = x.shape[0], x.shape[1]
    proj = x @ w_in
    aq, ak, av, dq, dk, dv, dz, dbeta, da = jnp.split(proj, SPLIT_POINTS, axis=-1)
    attn = _window_attention(aq.reshape(B, S, ATTN_HEADS, HEAD_DIM),
                             ak.reshape(B, S, ATTN_KV_HEADS, HEAD_DIM),
                             av.reshape(B, S, ATTN_KV_HEADS, HEAD_DIM), attn_sink, rel_bias)
    dn = _deltanet(dq, dk, dv, dz, dbeta, da, conv_w, a_log, dt_bias, dn_norm_w)
    mix = jnp.concatenate([attn, dn], axis=-1) @ w_out
    h = _layer_norm(DEEPNORM_ALPHA * x + mix, ln1_g, ln1_b)
    return _layer_norm(DEEPNORM_ALPHA * h + _peer(h, peer_wq, peer_keys, peer_u, peer_v), ln2_g, ln2_b)


def setup_inputs(seed: int = 0) -> dict:
    key = jax.random.key(seed)
    ks = jax.random.split(key, 17)
    f32 = jnp.float32
    nrm = lambda k, shape, scale: jax.random.normal(k, shape, f32) * scale
    x = nrm(ks[0], (BATCH, SEQ, D_MODEL), 1.0)
    w_in = nrm(ks[1], (DEPTH, D_MODEL, IN_WIDTH), D_MODEL ** -0.5)
    conv_w = nrm(ks[2], (DEPTH, CONV_WIDTH, CONV_CH), CONV_WIDTH ** -0.5)
    a_log = jnp.log(jax.random.uniform(ks[3], (DEPTH, 2, DN_HEADS), f32, 1.0, 16.0))
    dt = jnp.exp(jax.random.uniform(ks[4], (DEPTH, 2, DN_HEADS), f32, math.log(1e-3), math.log(1e-1)))
    dt_bias = dt + jnp.log(-jnp.expm1(-dt))
    dn_norm_w = 1.0 + nrm(ks[5], (DEPTH, DN_VAL_DIM), 0.02)
    attn_sink = nrm(ks[6], (DEPTH, ATTN_HEADS), 0.5)
    rel_bias = nrm(ks[7], (N_BUCKETS, ATTN_HEADS), 0.5)
    w_out = nrm(ks[8], (DEPTH, MIX_WIDTH, D_MODEL), MIX_WIDTH ** -0.5 * DEEPNORM_BETA)
    ln1_g = 1.0 + nrm(ks[9], (DEPTH, D_MODEL), 0.02)
    ln1_b = nrm(ks[10], (DEPTH, D_MODEL), 0.02)
    peer_wq = nrm(ks[11], (DEPTH, D_MODEL, PEER_HEADS * PEER_QDIM), D_MODEL ** -0.5)
    peer_keys = nrm(ks[12], (DEPTH, 2, PEER_NKEYS, PEER_HALF), PEER_HALF ** -0.5)
    peer_u = nrm(ks[13], (DEPTH, PEER_EXPERTS, D_MODEL), D_MODEL ** -0.5)
    peer_v = nrm(ks[14], (DEPTH, PEER_EXPERTS, D_MODEL), DEEPNORM_BETA * PEER_HEADS ** -0.5)
    ln2_g = 1.0 + nrm(ks[15], (DEPTH, D_MODEL), 0.02)
    ln2_b = nrm(ks[16], (DEPTH, D_MODEL), 0.02)
    return {"x": x, "w_in": w_in, "conv_w": conv_w, "a_log": a_log, "dt_bias": dt_bias,
            "dn_norm_w": dn_norm_w, "attn_sink": attn_sink, "rel_bias": rel_bias, "w_out": w_out,
            "ln1_g": ln1_g, "ln1_b": ln1_b, "peer_wq": peer_wq, "peer_keys": peer_keys,
            "peer_u": peer_u, "peer_v": peer_v, "ln2_g": ln2_g, "ln2_b": ln2_b}


def reference(x, w_in, conv_w, a_log, dt_bias, dn_norm_w, attn_sink, rel_bias, w_out,
              ln1_g, ln1_b, peer_wq, peer_keys, peer_u, peer_v, ln2_g, ln2_b):
    for l in range(DEPTH):
        x = _layer(x, w_in[l], conv_w[l], a_log[l], dt_bias[l], dn_norm_w[l], attn_sink[l], rel_bias,
                   w_out[l], ln1_g[l], ln1_b[l], peer_wq[l], peer_keys[l], peer_u[l], peer_v[l],
                   ln2_g[l], ln2_b[l])
    return x
```

```python
import functools
import math

import jax
import jax.numpy as jnp
import numpy as np
from jax import lax
from jax.experimental import pallas as pl
from jax.experimental.pallas import tpu as pltpu

F32 = jnp.float32
BF16 = jnp.bfloat16
I32 = jnp.int32

HEAD_DIM = 128
ATTN_HEADS = 8
ATTN_KV_HEADS = 2
ATTN_GROUP = ATTN_HEADS // ATTN_KV_HEADS
WINDOW = 128
ATTN_BLOCK = 128
N_BUCKETS = 32
MAX_DISTANCE = 128
DN_HEADS = 8
DN_DIM = 128
CONV_WIDTH = 5
DN_CHUNK = 128
PEER_HEADS = 8
PEER_NKEYS = 128
PEER_HALF = 128
PEER_TOPK = 16
DEPTH = 1
DEEPNORM_ALPHA = (2.0 * DEPTH) ** 0.25
LN_EPS = 1e-5
RMS_EPS = 1e-6
NEG_BIG = -1e30

COL_AQ = 0
COL_AK = COL_AQ + ATTN_HEADS * HEAD_DIM
COL_AV = COL_AK + ATTN_KV_HEADS * HEAD_DIM
COL_DQ = COL_AV + ATTN_KV_HEADS * HEAD_DIM
COL_DZ = COL_DQ + 3 * DN_HEADS * DN_DIM
COL_TAIL = COL_DZ + DN_HEADS * DN_DIM
N_TAIL = 4 * DN_HEADS

VMEM_LIMIT = 56 * 1024 * 1024


def _cparams(sem):
    return pltpu.CompilerParams(dimension_semantics=sem, vmem_limit_bytes=VMEM_LIMIT)


def _dot(a, b):
    return jnp.dot(a, b, preferred_element_type=F32)


def _dot_nt(a, b, precision=None):
    return lax.dot_general(a, b, (((1,), (1,)), ((), ())), preferred_element_type=F32, precision=precision)


def _inproj_kernel(x_ref, w_ref, o_ref, xb_ref):
    @pl.when(pl.program_id(1) == 0)
    def _():
        xb_ref[...] = x_ref[...].astype(BF16)

    o_ref[...] = _dot(xb_ref[...], w_ref[...]).astype(o_ref.dtype)


def _inproj(x2, w_main, tm, tn):
    M, K = x2.shape
    N = w_main.shape[1]
    return pl.pallas_call(
        _inproj_kernel,
        grid=(M // tm, N // tn),
        in_specs=[pl.BlockSpec((tm, K), lambda i, j: (i, 0)),
                  pl.BlockSpec((K, tn), lambda i, j: (0, j))],
        out_specs=pl.BlockSpec((tm, tn), lambda i, j: (i, j)),
        out_shape=jax.ShapeDtypeStruct((M, N), BF16),
        scratch_shapes=[pltpu.VMEM((tm, K), BF16)],
        compiler_params=_cparams(("parallel", "arbitrary")),
        name="inproj",
    )(x2, w_main)


def _tail_kernel(x_ref, wt_ref, wtT_ref, alog_r_ref, dtb_r_ref, alog_c_ref, dtb_c_ref, bg_ref, bgT_ref):
    xb = x_ref[...].astype(BF16)
    t = _dot(xb, wt_ref[...])
    tT = _dot_nt(wtT_ref[...], xb)
    lane = lax.broadcasted_iota(I32, t.shape, 1)
    g = -jnp.exp(alog_r_ref[...]) * jax.nn.softplus(t + dtb_r_ref[...])
    bg_ref[...] = jnp.where(lane < 2 * DN_HEADS, jax.nn.sigmoid(t), g)
    sub = lax.broadcasted_iota(I32, tT.shape, 0)
    gT = -jnp.exp(alog_c_ref[...]) * jax.nn.softplus(tT + dtb_c_ref[...])
    bgT_ref[...] = jnp.where(sub < 2 * DN_HEADS, jax.nn.sigmoid(tT), gT)


def _tail(x2, w_tail, a_log, dt_bias, tm):
    M, K = x2.shape
    wt = jnp.zeros((K, 128), BF16).at[:, :N_TAIL].set(w_tail.astype(BF16))
    wtT = w_tail.astype(BF16).T
    al = jnp.zeros((N_TAIL,), F32).at[2 * DN_HEADS:].set(a_log.reshape(-1))
    db = jnp.zeros((N_TAIL,), F32).at[2 * DN_HEADS:].set(dt_bias.reshape(-1))
    al_r = jnp.zeros((1, 128), F32).at[0, :N_TAIL].set(al)
    db_r = jnp.zeros((1, 128), F32).at[0, :N_TAIL].set(db)
    full = lambda shape: pl.BlockSpec(shape, lambda i: (0, 0))
    return pl.pallas_call(
        _tail_kernel,
        grid=(M // tm,),
        in_specs=[pl.BlockSpec((tm, K), lambda i: (i, 0)), full((K, 128)), full((N_TAIL, K)),
                  full((1, 128)), full((1, 128)), full((N_TAIL, 1)), full((N_TAIL, 1))],
        out_specs=[pl.BlockSpec((tm, 128), lambda i: (i, 0)), pl.BlockSpec((N_TAIL, tm), lambda i: (0, i))],
        out_shape=[jax.ShapeDtypeStruct((M, 128), F32), jax.ShapeDtypeStruct((N_TAIL, M), F32)],
        compiler_params=_cparams(("parallel",)),
        name="tail",
    )(x2, wt, wtT, al_r, db_r, al.reshape(N_TAIL, 1), db.reshape(N_TAIL, 1))


def _t5_bucket(rel):
    nb = N_BUCKETS // 2
    max_exact = nb // 2
    ret = jnp.where(rel > 0, nb, 0)
    n = jnp.abs(rel)
    large = max_exact + (jnp.log(jnp.maximum(n, 1).astype(F32) / max_exact)
                         / math.log(MAX_DISTANCE / max_exact) * (nb - max_exact)).astype(I32)
    large = jnp.minimum(large, nb - 1)
    return ret + jnp.where(n < max_exact, n, large)


def _attn_bias_table(rel_bias):
    qi = jnp.arange(ATTN_BLOCK)[:, None]
    kj = jnp.arange(3 * ATTN_BLOCK)[None, :]
    tabs = []
    for variant in range(3):
        rel = kj - qi - variant * ATTN_BLOCK
        b = rel_bias[_t5_bucket(rel)].astype(F32)
        b = jnp.where((jnp.abs(rel) <= WINDOW)[..., None], b, NEG_BIG)
        tabs.append(jnp.transpose(b, (2, 0, 1)))
    return jnp.stack(tabs)


def _attn_kernel(q_ref, k_ref, v_ref, bias_ref, sink_ref, o_ref, *, seq):
    n = pl.program_id(2)
    start = jnp.clip((n - 1) * ATTN_BLOCK, 0, seq - 3 * ATTN_BLOCK)
    start = pl.multiple_of(start, ATTN_BLOCK)
    kb = k_ref[pl.ds(start, 3 * ATTN_BLOCK), :]
    vb = v_ref[pl.ds(start, 3 * ATTN_BLOCK), :]
    for hh in range(ATTN_GROUP):
        q = q_ref[:, hh * HEAD_DIM:(hh + 1) * HEAD_DIM]
        s = _dot_nt(q, kb) * (HEAD_DIM ** -0.5) + bias_ref[hh]
        sk = sink_ref[hh:hh + 1, 0:1]
        m = jnp.maximum(jnp.max(s, axis=-1, keepdims=True), sk)
        p = jnp.exp(s - m)
        den = jnp.sum(p, axis=-1, keepdims=True) + jnp.exp(sk - m)
        o = _dot((p / den).astype(BF16), vb)
        o_ref[:, hh * HEAD_DIM:(hh + 1) * HEAD_DIM] = o.astype(o_ref.dtype)


def _attention(proj3, bias_tab, sink_b):
    B, S, _ = proj3.shape
    nb = S // ATTN_BLOCK
    assert nb >= 3
    gw = ATTN_GROUP * HEAD_DIM
    kcol = COL_AK // HEAD_DIM
    vcol = COL_AV // HEAD_DIM
    variant = lambda n: jnp.where(n == 0, 0, jnp.where(n == nb - 1, 2, 1))
    return pl.pallas_call(
        functools.partial(_attn_kernel, seq=S),
        grid=(B, ATTN_KV_HEADS, nb),
        in_specs=[pl.BlockSpec((None, ATTN_BLOCK, gw), lambda b, g, n: (b, n, g)),
                  pl.BlockSpec((None, S, HEAD_DIM), lambda b, g, n: (b, 0, kcol + g)),
                  pl.BlockSpec((None, S, HEAD_DIM), lambda b, g, n: (b, 0, vcol + g)),
                  pl.BlockSpec((None, ATTN_GROUP, ATTN_BLOCK, 3 * ATTN_BLOCK),
                               lambda b, g, n: (variant(n), g, 0, 0)),
                  pl.BlockSpec((None, ATTN_GROUP, 128), lambda b, g, n: (g, 0, 0))],
        out_specs=pl.BlockSpec((None, ATTN_BLOCK, gw), lambda b, g, n: (b, n, g)),
        out_shape=jax.ShapeDtypeStruct((B, S, ATTN_HEADS * HEAD_DIM), BF16),
        compiler_params=_cparams(("parallel", "parallel", "arbitrary")),
        name="window_attn",
    )(proj3, proj3, proj3, bias_tab, sink_b)


CONV_COLS = 4 * DN_DIM
CONV_HALO = 16


def _conv_kernel(x_ref, prev_ref, next_ref, w_ref, o_ref, buf_ref, *, ts, n_sblk):
    s = pl.program_id(1)
    c = pl.program_id(2)
    pad = CONV_WIDTH // 2
    prev = jnp.where(s > 0, prev_ref[...].astype(F32), 0.0)
    nxt = jnp.where(s < n_sblk - 1, next_ref[...].astype(F32), 0.0)
    buf_ref[0:CONV_HALO, :] = prev
    buf_ref[CONV_HALO:CONV_HALO + ts, :] = x_ref[...].astype(F32)
    buf_ref[CONV_HALO + ts:CONV_HALO + ts + CONV_HALO, :] = nxt
    y = jnp.zeros((ts, CONV_COLS), F32)
    for w in range(CONV_WIDTH):
        y = y + buf_ref[CONV_HALO + w - pad:CONV_HALO + w - pad + ts, :] * w_ref[w:w + 1, :]
    y = jax.nn.silu(y)
    is_qk = c < 4
    qscale = jnp.where(c < 2, DN_DIM ** -0.5, 1.0).astype(F32)
    for hh in range(CONV_COLS // DN_DIM):
        yh = y[:, hh * DN_DIM:(hh + 1) * DN_DIM]
        nrm = yh * lax.rsqrt(jnp.sum(yh * yh, axis=-1, keepdims=True) + RMS_EPS) * qscale
        o_ref[:, hh * DN_DIM:(hh + 1) * DN_DIM] = jnp.where(is_qk, nrm, yh)


def _conv(proj3, conv_w, ts):
    B, S, _ = proj3.shape
    n_sblk = S // ts
    c0 = COL_DQ // CONV_COLS
    ncb = 3 * DN_HEADS * DN_DIM // CONV_COLS
    hb = ts // CONV_HALO
    nh = S // CONV_HALO
    return pl.pallas_call(
        functools.partial(_conv_kernel, ts=ts, n_sblk=n_sblk),
        grid=(B, n_sblk, ncb),
        in_specs=[pl.BlockSpec((None, ts, CONV_COLS), lambda b, s, c: (b, s, c0 + c)),
                  pl.BlockSpec((None, CONV_HALO, CONV_COLS),
                               lambda b, s, c: (b, jnp.maximum(s * hb - 1, 0), c0 + c)),
                  pl.BlockSpec((None, CONV_HALO, CONV_COLS),
                               lambda b, s, c: (b, jnp.minimum((s + 1) * hb, nh - 1), c0 + c)),
                  pl.BlockSpec((CONV_WIDTH, CONV_COLS), lambda b, s, c: (0, c))],
        out_specs=pl.BlockSpec((None, ts, CONV_COLS), lambda b, s, c: (b, s, c)),
        out_shape=jax.ShapeDtypeStruct((B, S, 3 * DN_HEADS * DN_DIM), F32),
        scratch_shapes=[pltpu.VMEM((ts + 2 * CONV_HALO, CONV_COLS), F32)],
        compiler_params=_cparams(("parallel", "parallel", "arbitrary")),
        name="short_conv",
    )(proj3, proj3, proj3, conv_w)


def _dn_direction(d, h, q_ref, k_ref, v_ref, bg_ref, bgT_ref, o_ref, state_ref):
    C = DN_CHUNK
    hi = lax.Precision.HIGHEST
    row = lax.broadcasted_iota(I32, (C, C), 0)
    col = lax.broadcasted_iota(I32, (C, C), 1)
    if d == 0:
        incl, strict, last = row >= col, row > col, C - 1
    else:
        incl, strict, last = row <= col, row < col, 0
    tri = incl.astype(F32)
    bg = bg_ref[...]
    g_cum = jnp.dot(tri, bg, preferred_element_type=F32, precision=hi)
    g_cum_t = _dot_nt(bgT_ref[...], tri, precision=hi)
    cb = d * DN_HEADS + h
    cg = 2 * DN_HEADS + d * DN_HEADS + h
    beta_c = jnp.sum(jnp.where(col == cb, bg, 0.0), axis=1, keepdims=True)
    g_c = jnp.sum(jnp.where(col == cg, g_cum, 0.0), axis=1, keepdims=True)
    sub = lax.broadcasted_iota(I32, g_cum_t.shape, 0)
    g_r = jnp.sum(jnp.where(sub == cg, g_cum_t, 0.0), axis=0, keepdims=True)
    decay = jnp.where(incl, jnp.exp(jnp.where(incl, g_c - g_r, 0.0)), 0.0)

    q = q_ref[...]
    k = k_ref[...]
    v = v_ref[...]
    kbeta = k * beta_c
    k_b = k.astype(BF16)
    lmat = jnp.where(strict, _dot_nt(kbeta.astype(BF16), k_b) * decay, 0.0)
    x = jnp.where(row == col, 1.0, 0.0)
    half = 1
    while half < C:
        lo, hi_ = (col, row) if d == 0 else (row, col)
        sel = ((row ^ col) < 2 * half) & ((hi_ & half) != 0) & ((lo & half) == 0)
        lm = jnp.where(sel, lmat, 0.0)
        if half == 1:
            x = x - lm
        else:
            xb = x.astype(BF16)
            x = x - _dot(_dot(xb, lm.astype(BF16)).astype(BF16), xb)
        half *= 2
    e_g = jnp.exp(g_c)
    rhs = jnp.concatenate([v * beta_c, kbeta * e_g], axis=1).astype(BF16)
    uw = _dot(x.astype(BF16), rhs)
    u = uw[:, :DN_DIM]
    w = uw[:, DN_DIM:]
    amat = _dot_nt(q.astype(BF16), k_b) * decay
    qg = q * e_g
    g_last = g_c[last:last + 1, :]
    kd = k * jnp.exp(g_last - g_c)
    state = state_ref[d]
    state_b = state.astype(BF16)
    v_new = u - _dot(w.astype(BF16), state_b)
    v_new_b = v_new.astype(BF16)
    o_ref[...] = _dot(qg.astype(BF16), state_b) + _dot(amat.astype(BF16), v_new_b)
    state_ref[d] = state * jnp.exp(g_last) + _dot(kd.T.astype(BF16), v_new_b)


def _dn_kernel(qf, kf, vf, bgf, bgTf, qb, kb, vb, bgb, bgTb, of_ref, ob_ref, state_ref):
    h = pl.program_id(1)

    @pl.when(pl.program_id(2) == 0)
    def _():
        state_ref[...] = jnp.zeros_like(state_ref)

    _dn_direction(0, h, qf, kf, vf, bgf, bgTf, of_ref, state_ref)
    _dn_direction(1, h, qb, kb, vb, bgb, bgTb, ob_ref, state_ref)


def _deltanet(qkv, bg3, bgT):
    B, S, _ = qkv.shape
    C = DN_CHUNK
    N = S // C
    H = DN_HEADS

    def specs(nmap):
        return [pl.BlockSpec((None, C, DN_DIM), lambda b, h, n: (b, nmap(n), h)),
                pl.BlockSpec((None, C, DN_DIM), lambda b, h, n: (b, nmap(n), H + h)),
                pl.BlockSpec((None, C, DN_DIM), lambda b, h, n: (b, nmap(n), 2 * H + h)),
                pl.BlockSpec((None, C, 128), lambda b, h, n: (b, nmap(n), 0)),
                pl.BlockSpec((N_TAIL, C), lambda b, h, n: (0, b * N + nmap(n)))]

    fwd = lambda n: n
    bwd = lambda n: N - 1 - n
    out_sd = jax.ShapeDtypeStruct((B, S, H * DN_DIM), F32)
    return pl.pallas_call(
        _dn_kernel,
        grid=(B, H, N),
        in_specs=specs(fwd) + specs(bwd),
        out_specs=[pl.BlockSpec((None, C, DN_DIM), lambda b, h, n: (b, fwd(n), h)),
                   pl.BlockSpec((None, C, DN_DIM), lambda b, h, n: (b, bwd(n), h))],
        out_shape=[out_sd, out_sd],
        scratch_shapes=[pltpu.VMEM((2, DN_DIM, DN_DIM), F32)],
        compiler_params=_cparams(("parallel", "parallel", "arbitrary")),
        name="deltanet_scan",
    )(qkv, qkv, qkv, bg3, bgT, qkv, qkv, qkv, bg3, bgT)


def _layer_norm(y, g, b):
    mu = jnp.mean(y, axis=-1, keepdims=True)
    var = jnp.mean(jnp.square(y - mu), axis=-1, keepdims=True)
    return (y - mu) * lax.rsqrt(var + LN_EPS) * g + b


def _outproj_kernel(attn_ref, of_ref, ob_ref, z0_ref, z1_ref, x_ref, w_ref, nw_ref, g_ref, b_ref, h_ref, mix_ref):
    aw = ATTN_HEADS * HEAD_DIM
    mix_ref[:, :aw] = attn_ref[...]
    nw = nw_ref[...]
    half = z0_ref.shape[1] // DN_DIM
    for hh in range(DN_HEADS):
        sl = slice(hh * DN_DIM, (hh + 1) * DN_DIM)
        o = of_ref[:, sl] + ob_ref[:, sl]
        z_ref = z0_ref if hh < half else z1_ref
        zc = (hh % half) * DN_DIM
        z = z_ref[:, zc:zc + DN_DIM].astype(F32)
        o = o * lax.rsqrt(jnp.mean(o * o, axis=-1, keepdims=True) + RMS_EPS) * nw
        mix_ref[:, aw + hh * DN_DIM:aw + (hh + 1) * DN_DIM] = (o * jax.nn.silu(z)).astype(BF16)
    y = DEEPNORM_ALPHA * x_ref[...] + _dot(mix_ref[...], w_ref[...])
    h_ref[...] = _layer_norm(y, g_ref[...], b_ref[...])


def _outproj(attn2, of2, ob2, proj2, x2, w_out_b, norm_w, ln_g, ln_b, tm):
    M, D = x2.shape
    dw = DN_HEADS * DN_DIM
    zb = dw // 2
    z0 = COL_DZ // zb
    row = lambda w: pl.BlockSpec((tm, w), lambda i: (i, 0))
    full = lambda shape: pl.BlockSpec(shape, lambda i: (0, 0))
    return pl.pallas_call(
        _outproj_kernel,
        grid=(M // tm,),
        in_specs=[row(ATTN_HEADS * HEAD_DIM), row(dw), row(dw),
                  pl.BlockSpec((tm, zb), lambda i: (i, z0)), pl.BlockSpec((tm, zb), lambda i: (i, z0 + 1)),
                  row(D), full(w_out_b.shape), full((1, DN_DIM)), full((1, D)), full((1, D))],
        out_specs=row(D),
        out_shape=jax.ShapeDtypeStruct((M, D), F32),
        scratch_shapes=[pltpu.VMEM((tm, ATTN_HEADS * HEAD_DIM + dw), BF16)],
        compiler_params=_cparams(("parallel",)),
        name="outproj_ln",
    )(attn2, of2, ob2, proj2, proj2, x2, w_out_b, norm_w.reshape(1, -1), ln_g.reshape(1, -1), ln_b.reshape(1, -1))


TOPK_W = 128


def _topk_rounds(s, val_ref, idx_ref):
    iota = lax.broadcasted_iota(I32, s.shape, 0)
    for r in range(PEER_TOPK):
        m = jnp.max(s, axis=0, keepdims=True)
        idx = jnp.min(jnp.where(s == m, iota, s.shape[0]), axis=0, keepdims=True)
        val_ref[r:r + 1, :] = m
        idx_ref[r:r + 1, :] = idx
        s = jnp.where(iota == idx, -jnp.inf, s)


def _pick_rows(sel, table):
    out = jnp.zeros(sel.shape, table.dtype)
    for a in range(PEER_TOPK):
        out = jnp.where(sel == a, table[a:a + 1, :], out)
    return out


def _peer_topk_kernel(h_ref, wq_ref, keys_ref, i_ref, j_ref, g_ref,
                      qs_ref, sv_ref, si_ref, cand_ref, tv_ref, tc_ref, *, tq):
    hd = pl.program_id(1)
    nq = 2 * PEER_HEADS

    @pl.when(hd == 0)
    def _():
        q = _dot(h_ref[...].astype(BF16), wq_ref[...])
        for c in range(nq):
            qs_ref[c] = q[:, c * PEER_HALF:(c + 1) * PEER_HALF].astype(BF16)

    for slab in range(tq // TOPK_W):
        lanes = slice(slab * TOPK_W, (slab + 1) * TOPK_W)
        for p in range(2):
            qhp = qs_ref[2 * hd + p, lanes, :]
            s = _dot_nt(keys_ref[p], qhp)
            _topk_rounds(s, sv_ref.at[p], si_ref.at[p])
        sv2 = sv_ref[1]
        for a in range(PEER_TOPK):
            cand_ref[a * PEER_TOPK:(a + 1) * PEER_TOPK, :] = sv_ref[0, a:a + 1, :] + sv2
        _topk_rounds(cand_ref[...], tv_ref, tc_ref)
        top_s = tv_ref[...]
        top_c = tc_ref[...]
        e = jnp.exp(top_s - top_s[0:1, :])
        g_ref[:, lanes] = e / jnp.sum(e, axis=0, keepdims=True)
        shift = int(math.log2(PEER_TOPK))
        i_ref[:, lanes] = _pick_rows(top_c >> shift, si_ref[0])
        j_ref[:, lanes] = _pick_rows(top_c & (PEER_TOPK - 1), si_ref[1])


def _peer_topk(h2, wq_b, keys_b, tq):
    M, D = h2.shape
    nq = 2 * PEER_HEADS
    out_spec = pl.BlockSpec((PEER_TOPK, tq), lambda i, hd: (hd, i))
    np_ = PEER_HEADS * PEER_TOPK
    return pl.pallas_call(
        functools.partial(_peer_topk_kernel, tq=tq),
        grid=(M // tq, PEER_HEADS),
        in_specs=[pl.BlockSpec((tq, D), lambda i, hd: (i, 0)),
                  pl.BlockSpec(wq_b.shape, lambda i, hd: (0, 0)),
                  pl.BlockSpec(keys_b.shape, lambda i, hd: (0, 0, 0))],
        out_specs=[out_spec, out_spec, out_spec],
        out_shape=[jax.ShapeDtypeStruct((np_, M), I32), jax.ShapeDtypeStruct((np_, M), I32),
                   jax.ShapeDtypeStruct((np_, M), F32)],
        scratch_shapes=[pltpu.VMEM((nq, tq, PEER_HALF), BF16),
                        pltpu.VMEM((2, PEER_TOPK, TOPK_W), F32), pltpu.VMEM((2, PEER_TOPK, TOPK_W), I32),
                        pltpu.VMEM((PEER_TOPK * PEER_TOPK, TOPK_W), F32),
                        pltpu.VMEM((PEER_TOPK, TOPK_W), F32), pltpu.VMEM((PEER_TOPK, TOPK_W), I32)],
        compiler_params=_cparams(("parallel", "arbitrary")),
        name="peer_topk",
    )(h2, wq_b, keys_b)


def _gate_kernel(i_ref, j_ref, g_ref, o_ref, *, tg):
    sub = lax.broadcasted_iota(I32, (PEER_NKEYS, PEER_HEADS * PEER_TOPK), 0)

    def body(t, carry):
        irow = i_ref[pl.ds(t, 1), :]
        jrow = j_ref[pl.ds(t, 1), :]
        grow = g_ref[pl.ds(t, 1), :]
        at = jnp.where(sub == irow, grow, 0.0).astype(BF16)
        bt = jnp.where(sub == jrow, 1.0, 0.0).astype(BF16)
        o_ref[t] = _dot_nt(at, bt).astype(o_ref.dtype)
        return carry

    lax.fori_loop(0, tg, body, 0)


def _gate_matrix(i_n, j_n, g_n, tg):
    M, P = i_n.shape
    spec = pl.BlockSpec((tg, P), lambda i: (i, 0))
    return pl.pallas_call(
        functools.partial(_gate_kernel, tg=tg),
        grid=(M // tg,),
        in_specs=[spec, spec, spec],
        out_specs=pl.BlockSpec((tg, PEER_NKEYS, PEER_NKEYS), lambda i: (i, 0, 0)),
        out_shape=jax.ShapeDtypeStruct((M, PEER_NKEYS, PEER_NKEYS), BF16),
        compiler_params=_cparams(("parallel",)),
        name="peer_gates",
    )(i_n, j_n, g_n)


def _peer_dense_kernel(h_ref, u_ref, v_ref, gm_ref, g_ref, b_ref, o_ref, hb_ref):
    j = pl.program_id(1)

    @pl.when(j == 0)
    def _():
        hb_ref[...] = h_ref[...].astype(BF16)
        o_ref[...] = jnp.zeros_like(o_ref)

    s = _dot_nt(hb_ref[...], u_ref[...])
    act = 0.5 * s * (1.0 + lax.erf(s * (2.0 ** -0.5)))
    o_ref[...] += _dot((act * gm_ref[...].astype(F32)).astype(BF16), v_ref[...])

    @pl.when(j == pl.num_programs(1) - 1)
    def _():
        y = DEEPNORM_ALPHA * h_ref[...] + o_ref[...]
        o_ref[...] = _layer_norm(y, g_ref[...], b_ref[...])


def _peer_dense(h2, u_b, v_b, gmat, ln_g, ln_b, tt, eb):
    M, D = h2.shape
    E = u_b.shape[0]
    return pl.pallas_call(
        _peer_dense_kernel,
        grid=(M // tt, E // eb),
        in_specs=[pl.BlockSpec((tt, D), lambda i, j: (i, 0)),
                  pl.BlockSpec((eb, D), lambda i, j: (j, 0)),
                  pl.BlockSpec((eb, D), lambda i, j: (j, 0)),
                  pl.BlockSpec((tt, eb), lambda i, j: (i, j)),
                  pl.BlockSpec((1, D), lambda i, j: (0, 0)),
                  pl.BlockSpec((1, D), lambda i, j: (0, 0))],
        out_specs=pl.BlockSpec((tt, D), lambda i, j: (i, 0)),
        out_shape=jax.ShapeDtypeStruct((M, D), F32),
        scratch_shapes=[pltpu.VMEM((tt, D), BF16)],
        compiler_params=_cparams(("parallel", "arbitrary")),
        name="peer_dense",
    )(h2, u_b, v_b, gmat, ln_g.reshape(1, -1), ln_b.reshape(1, -1))


def _tile(n, pref):
    t = min(n, pref)
    assert n % t == 0
    return t


def _layer(x, w_in, conv_w, a_log, dt_bias, dn_norm_w, attn_sink, rel_bias, w_out,
           ln1_g, ln1_b, peer_wq, peer_keys, peer_u, peer_v, ln2_g, ln2_b):
    B, S, D = x.shape
    T = B * S
    x2 = x.reshape(T, D)

    proj = _inproj(x2, w_in[:, :COL_TAIL].astype(BF16), _tile(T, 1024), 512)
    bg, bgT = _tail(x2, w_in[:, COL_TAIL:], a_log, dt_bias, _tile(T, 1024))
    proj3 = proj.reshape(B, S, -1)

    sink_b = jnp.broadcast_to(attn_sink.astype(F32).reshape(ATTN_KV_HEADS, ATTN_GROUP, 1),
                              (ATTN_KV_HEADS, ATTN_GROUP, 128))
    attn = _attention(proj3, _attn_bias_table(rel_bias), sink_b)

    qkv = _conv(proj3, conv_w, _tile(S, 512))
    o_f, o_b = _deltanet(qkv, bg.reshape(B, S, 128), bgT)

    h = _outproj(attn.reshape(T, -1), o_f.reshape(T, -1), o_b.reshape(T, -1), proj, x2,
                 w_out.astype(BF16), dn_norm_w, ln1_g, ln1_b, _tile(T, 512))

    i_t, j_t, g_t = _peer_topk(h, peer_wq.astype(BF16), peer_keys.astype(BF16), _tile(T, 256))
    gmat = _gate_matrix(i_t.T, j_t.T, g_t.T, _tile(T, 64))
    y = _peer_dense(h, peer_u.astype(BF16), peer_v.astype(BF16), gmat.reshape(T, -1), ln2_g, ln2_b,
                    _tile(T, 512), 512)
    return y.reshape(B, S, D)


def kernel(x, w_in, conv_w, a_log, dt_bias, dn_norm_w, attn_sink, rel_bias, w_out, ln1_g, ln1_b,
           peer_wq, peer_keys, peer_u, peer_v, ln2_g, ln2_b):
    for l in range(DEPTH):
        x = _layer(x, w_in[l], conv_w[l], a_log[l], dt_bias[l], dn_norm_w[l], attn_sink[l], rel_bias,
                   w_out[l], ln1_g[l], ln1_b[l], peer_wq[l], peer_keys[l], peer_u[l], peer_v[l],
                   ln2_g[l], ln2_b[l])
    return x
```

```python
import functools
import math

import jax
import jax.numpy as jnp
import numpy as np
from jax import lax
from jax.experimental import pallas as pl
from jax.experimental.pallas import tpu as pltpu

F32 = jnp.float32
BF16 = jnp.bfloat16
I32 = jnp.int32

HEAD_DIM = 128
ATTN_HEADS = 8
ATTN_KV_HEADS = 2
ATTN_GROUP = ATTN_HEADS // ATTN_KV_HEADS
WINDOW = 128
ATTN_BLOCK = 128
N_BUCKETS = 32
MAX_DISTANCE = 128
DN_HEADS = 8
DN_DIM = 128
CONV_WIDTH = 5
DN_CHUNK = 128
PEER_HEADS = 8
PEER_NKEYS = 128
PEER_HALF = 128
PEER_TOPK = 16
DEPTH = 1
DEEPNORM_ALPHA = (2.0 * DEPTH) ** 0.25
LN_EPS = 1e-5
RMS_EPS = 1e-6
NEG_BIG = -1e30

COL_AQ = 0
COL_AK = COL_AQ + ATTN_HEADS * HEAD_DIM
COL_AV = COL_AK + ATTN_KV_HEADS * HEAD_DIM
COL_DQ = COL_AV + ATTN_KV_HEADS * HEAD_DIM
COL_DZ = COL_DQ + 3 * DN_HEADS * DN_DIM
COL_TAIL = COL_DZ + DN_HEADS * DN_DIM
N_TAIL = 4 * DN_HEADS

VMEM_LIMIT = 56 * 1024 * 1024


def _cparams(sem):
    return pltpu.CompilerParams(dimension_semantics=sem, vmem_limit_bytes=VMEM_LIMIT)


def _dot(a, b):
    return jnp.dot(a, b, preferred_element_type=F32)


def _dot_nt(a, b, precision=None):
    return lax.dot_general(a, b, (((1,), (1,)), ((), ())), preferred_element_type=F32, precision=precision)


def _inproj_kernel(x_ref, w_ref, o_ref, xb_ref):
    @pl.when(pl.program_id(1) == 0)
    def _():
        xb_ref[...] = x_ref[...].astype(BF16)

    o_ref[...] = _dot(xb_ref[...], w_ref[...]).astype(o_ref.dtype)


def _inproj(x2, w_main, tm, tn):
    M, K = x2.shape
    N = w_main.shape[1]
    return pl.pallas_call(
        _inproj_kernel,
        grid=(M // tm, N // tn),
        in_specs=[pl.BlockSpec((tm, K), lambda i, j: (i, 0)),
                  pl.BlockSpec((K, tn), lambda i, j: (0, j))],
        out_specs=pl.BlockSpec((tm, tn), lambda i, j: (i, j)),
        out_shape=jax.ShapeDtypeStruct((M, N), BF16),
        scratch_shapes=[pltpu.VMEM((tm, K), BF16)],
        compiler_params=_cparams(("parallel", "arbitrary")),
        name="inproj",
    )(x2, w_main)


def _tail_kernel(x_ref, wt_ref, wtT_ref, alog_r_ref, dtb_r_ref, alog_c_ref, dtb_c_ref, bg_ref, bgT_ref):
    xb = x_ref[...].astype(BF16)
    t = _dot(xb, wt_ref[...])
    tT = _dot_nt(wtT_ref[...], xb)
    lane = lax.broadcasted_iota(I32, t.shape, 1)
    g = -jnp.exp(alog_r_ref[...]) * jax.nn.softplus(t + dtb_r_ref[...])
    bg_ref[...] = jnp.where(lane < 2 * DN_HEADS, jax.nn.sigmoid(t), g)
    sub = lax.broadcasted_iota(I32, tT.shape, 0)
    gT = -jnp.exp(alog_c_ref[...]) * jax.nn.softplus(tT + dtb_c_ref[...])
    bgT_ref[...] = jnp.where(sub < 2 * DN_HEADS, jax.nn.sigmoid(tT), gT)


def _tail(x2, w_tail, a_log, dt_bias, tm):
    M, K = x2.shape
    wt = jnp.zeros((K, 128), BF16).at[:, :N_TAIL].set(w_tail.astype(BF16))
    wtT = w_tail.astype(BF16).T
    al = jnp.zeros((N_TAIL,), F32).at[2 * DN_HEADS:].set(a_log.reshape(-1))
    db = jnp.zeros((N_TAIL,), F32).at[2 * DN_HEADS:].set(dt_bias.reshape(-1))
    al_r = jnp.zeros((1, 128), F32).at[0, :N_TAIL].set(al)
    db_r = jnp.zeros((1, 128), F32).at[0, :N_TAIL].set(db)
    full = lambda shape: pl.BlockSpec(shape, lambda i: (0, 0))
    return pl.pallas_call(
        _tail_kernel,
        grid=(M // tm,),
        in_specs=[pl.BlockSpec((tm, K), lambda i: (i, 0)), full((K, 128)), full((N_TAIL, K)),
                  full((1, 128)), full((1, 128)), full((N_TAIL, 1)), full((N_TAIL, 1))],
        out_specs=[pl.BlockSpec((tm, 128), lambda i: (i, 0)), pl.BlockSpec((N_TAIL, tm), lambda i: (0, i))],
        out_shape=[jax.ShapeDtypeStruct((M, 128), F32), jax.ShapeDtypeStruct((N_TAIL, M), F32)],
        compiler_params=_cparams(("parallel",)),
        name="tail",
    )(x2, wt, wtT, al_r, db_r, al.reshape(N_TAIL, 1), db.reshape(N_TAIL, 1))


def _t5_bucket(rel):
    nb = N_BUCKETS // 2
    max_exact = nb // 2
    ret = jnp.where(rel > 0, nb, 0)
    n = jnp.abs(rel)
    large = max_exact + (jnp.log(jnp.maximum(n, 1).astype(F32) / max_exact)
                         / math.log(MAX_DISTANCE / max_exact) * (nb - max_exact)).astype(I32)
    large = jnp.minimum(large, nb - 1)
    return ret + jnp.where(n < max_exact, n, large)


def _attn_bias_table(rel_bias):
    qi = jnp.arange(ATTN_BLOCK)[:, None]
    kj = jnp.arange(3 * ATTN_BLOCK)[None, :]
    rel = jnp.stack([kj - qi - variant * ATTN_BLOCK for variant in range(3)])
    bucket = jnp.where(jnp.abs(rel) <= WINDOW, _t5_bucket(rel), -1)[:, None]
    tab = jnp.full((3, ATTN_HEADS, ATTN_BLOCK, 3 * ATTN_BLOCK), NEG_BIG, F32)
    rb = rel_bias.astype(F32)
    for bk in range(N_BUCKETS):
        tab = jnp.where(bucket == bk, rb[bk][None, :, None, None], tab)
    return tab


def _attn_kernel(q_ref, k_ref, v_ref, bias_ref, sink_ref, o_ref, *, seq):
    n = pl.program_id(2)
    start = jnp.clip((n - 1) * ATTN_BLOCK, 0, seq - 3 * ATTN_BLOCK)
    start = pl.multiple_of(start, ATTN_BLOCK)
    kb = k_ref[pl.ds(start, 3 * ATTN_BLOCK), :]
    vb = v_ref[pl.ds(start, 3 * ATTN_BLOCK), :]
    for hh in range(ATTN_GROUP):
        q = q_ref[:, hh * HEAD_DIM:(hh + 1) * HEAD_DIM]
        s = _dot_nt(q, kb) * (HEAD_DIM ** -0.5) + bias_ref[hh]
        sk = sink_ref[hh:hh + 1, 0:1]
        m = jnp.maximum(jnp.max(s, axis=-1, keepdims=True), sk)
        p = jnp.exp(s - m)
        den = jnp.sum(p, axis=-1, keepdims=True) + jnp.exp(sk - m)
        o = _dot((p / den).astype(BF16), vb)
        o_ref[:, hh * HEAD_DIM:(hh + 1) * HEAD_DIM] = o.astype(o_ref.dtype)


def _attention(proj3, bias_tab, sink_b):
    B, S, _ = proj3.shape
    nb = S // ATTN_BLOCK
    assert nb >= 3
    gw = ATTN_GROUP * HEAD_DIM
    kcol = COL_AK // HEAD_DIM
    vcol = COL_AV // HEAD_DIM
    variant = lambda n: jnp.where(n == 0, 0, jnp.where(n == nb - 1, 2, 1))
    return pl.pallas_call(
        functools.partial(_attn_kernel, seq=S),
        grid=(B, ATTN_KV_HEADS, nb),
        in_specs=[pl.BlockSpec((None, ATTN_BLOCK, gw), lambda b, g, n: (b, n, g)),
                  pl.BlockSpec((None, S, HEAD_DIM), lambda b, g, n: (b, 0, kcol + g)),
                  pl.BlockSpec((None, S, HEAD_DIM), lambda b, g, n: (b, 0, vcol + g)),
                  pl.BlockSpec((None, ATTN_GROUP, ATTN_BLOCK, 3 * ATTN_BLOCK),
                               lambda b, g, n: (variant(n), g, 0, 0)),
                  pl.BlockSpec((None, ATTN_GROUP, 128), lambda b, g, n: (g, 0, 0))],
        out_specs=pl.BlockSpec((None, ATTN_BLOCK, gw), lambda b, g, n: (b, n, g)),
        out_shape=jax.ShapeDtypeStruct((B, S, ATTN_HEADS * HEAD_DIM), BF16),
        compiler_params=_cparams(("parallel", "parallel", "arbitrary")),
        name="window_attn",
    )(proj3, proj3, proj3, bias_tab, sink_b)


CONV_COLS = 4 * DN_DIM
CONV_HALO = 16


def _conv_kernel(x_ref, prev_ref, next_ref, w_ref, o_ref, buf_ref, *, ts, n_sblk):
    s = pl.program_id(1)
    c = pl.program_id(2)
    pad = CONV_WIDTH // 2
    prev = jnp.where(s > 0, prev_ref[...].astype(F32), 0.0)
    nxt = jnp.where(s < n_sblk - 1, next_ref[...].astype(F32), 0.0)
    buf_ref[0:CONV_HALO, :] = prev
    buf_ref[CONV_HALO:CONV_HALO + ts, :] = x_ref[...].astype(F32)
    buf_ref[CONV_HALO + ts:CONV_HALO + ts + CONV_HALO, :] = nxt
    y = jnp.zeros((ts, CONV_COLS), F32)
    for w in range(CONV_WIDTH):
        y = y + buf_ref[CONV_HALO + w - pad:CONV_HALO + w - pad + ts, :] * w_ref[w:w + 1, :]
    y = jax.nn.silu(y)
    is_qk = c < 4
    qscale = jnp.where(c < 2, DN_DIM ** -0.5, 1.0).astype(F32)
    for hh in range(CONV_COLS // DN_DIM):
        yh = y[:, hh * DN_DIM:(hh + 1) * DN_DIM]
        nrm = yh * lax.rsqrt(jnp.sum(yh * yh, axis=-1, keepdims=True) + RMS_EPS) * qscale
        o_ref[:, hh * DN_DIM:(hh + 1) * DN_DIM] = jnp.where(is_qk, nrm, yh)


def _conv(proj3, conv_w, ts):
    B, S, _ = proj3.shape
    n_sblk = S // ts
    c0 = COL_DQ // CONV_COLS
    ncb = 3 * DN_HEADS * DN_DIM // CONV_COLS
    hb = ts // CONV_HALO
    nh = S // CONV_HALO
    return pl.pallas_call(
        functools.partial(_conv_kernel, ts=ts, n_sblk=n_sblk),
        grid=(B, n_sblk, ncb),
        in_specs=[pl.BlockSpec((None, ts, CONV_COLS), lambda b, s, c: (b, s, c0 + c)),
                  pl.BlockSpec((None, CONV_HALO, CONV_COLS),
                               lambda b, s, c: (b, jnp.maximum(s * hb - 1, 0), c0 + c)),
                  pl.BlockSpec((None, CONV_HALO, CONV_COLS),
                               lambda b, s, c: (b, jnp.minimum((s + 1) * hb, nh - 1), c0 + c)),
                  pl.BlockSpec((CONV_WIDTH, CONV_COLS), lambda b, s, c: (0, c))],
        out_specs=pl.BlockSpec((None, ts, CONV_COLS), lambda b, s, c: (b, s, c)),
        out_shape=jax.ShapeDtypeStruct((B, S, 3 * DN_HEADS * DN_DIM), F32),
        scratch_shapes=[pltpu.VMEM((ts + 2 * CONV_HALO, CONV_COLS), F32)],
        compiler_params=_cparams(("parallel", "parallel", "arbitrary")),
        name="short_conv",
    )(proj3, proj3, proj3, conv_w)


def _dn_direction(d, q_ref, k_ref, v_ref, bg_ref, bgT_ref, o_ref, state_ref):
    C = DN_CHUNK
    hi = lax.Precision.HIGHEST
    row = lax.broadcasted_iota(I32, (C, C), 0)
    col = lax.broadcasted_iota(I32, (C, C), 1)
    if d == 0:
        incl, strict, last, lo, up = row >= col, row > col, C - 1, col, row
    else:
        incl, strict, last, lo, up = row <= col, row < col, 0, row, col
    tri = incl.astype(F32)
    eye = jnp.where(row == col, 1.0, 0.0)
    sels = []
    half = 1
    while half < C:
        sels.append(((row ^ col) < 2 * half) & ((up & half) != 0) & ((lo & half) == 0))
        half *= 2
    bg = bg_ref[...]
    g_cum = jnp.dot(tri, bg, preferred_element_type=F32, precision=hi)
    g_cum_t = _dot_nt(bgT_ref[...], tri, precision=hi)
    H = range(DN_HEADS)
    hs = [slice(h * DN_DIM, (h + 1) * DN_DIM) for h in H]
    beta_c = [bg[:, d * DN_HEADS + h:d * DN_HEADS + h + 1] for h in H]
    g_c = [g_cum[:, 2 * DN_HEADS + d * DN_HEADS + h:2 * DN_HEADS + d * DN_HEADS + h + 1] for h in H]
    g_r = [g_cum_t[2 * DN_HEADS + d * DN_HEADS + h:2 * DN_HEADS + d * DN_HEADS + h + 1, :] for h in H]
    decay = [jnp.where(incl, jnp.exp(jnp.where(incl, g_c[h] - g_r[h], 0.0)), 0.0) for h in H]
    kbeta = [k_ref[:, hs[h]] * beta_c[h] for h in H]
    lmat = [jnp.where(strict, _dot_nt(kbeta[h].astype(BF16), k_ref[:, hs[h]].astype(BF16)) * decay[h], 0.0)
            for h in H]
    x = [eye - jnp.where(sels[0], lmat[h], 0.0) for h in H]
    for sel in sels[1:]:
        xb = [x[h].astype(BF16) for h in H]
        t = [_dot(xb[h], jnp.where(sel, lmat[h], 0.0).astype(BF16)) for h in H]
        x = [x[h] - _dot(t[h].astype(BF16), xb[h]) for h in H]
    e_g = [jnp.exp(g_c[h]) for h in H]
    uw = [_dot(x[h].astype(BF16),
               jnp.concatenate([v_ref[:, hs[h]] * beta_c[h], kbeta[h] * e_g[h]], axis=1).astype(BF16)) for h in H]
    amat = [(_dot_nt(q_ref[:, hs[h]].astype(BF16), k_ref[:, hs[h]].astype(BF16)) * decay[h]).astype(BF16) for h in H]
    g_last = [g_c[h][last:last + 1, :] for h in H]
    state_b = [state_ref[d, h].astype(BF16) for h in H]
    v_new = [(uw[h][:, :DN_DIM] - _dot(uw[h][:, DN_DIM:].astype(BF16), state_b[h])).astype(BF16) for h in H]
    for h in H:
        qg = (q_ref[:, hs[h]] * e_g[h]).astype(BF16)
        o_ref[:, hs[h]] = _dot(qg, state_b[h]) + _dot(amat[h], v_new[h])
    for h in H:
        kd = k_ref[:, hs[h]] * jnp.exp(g_last[h] - g_c[h])
        state_ref[d, h] = state_ref[d, h] * jnp.exp(g_last[h]) + _dot(kd.T.astype(BF16), v_new[h])


def _dn_kernel(qf, kf, vf, bgf, bgTf, qb, kb, vb, bgb, bgTb, of_ref, ob_ref, state_ref):
    @pl.when(pl.program_id(1) == 0)
    def _():
        state_ref[...] = jnp.zeros_like(state_ref)

    _dn_direction(0, qf, kf, vf, bgf, bgTf, of_ref, state_ref)
    _dn_direction(1, qb, kb, vb, bgb, bgTb, ob_ref, state_ref)


def _deltanet(qkv, bg3, bgT):
    B, S, _ = qkv.shape
    C = DN_CHUNK
    N = S // C
    hw = DN_HEADS * DN_DIM

    def specs(nmap):
        return [pl.BlockSpec((None, C, hw), lambda b, n: (b, nmap(n), 0)),
                pl.BlockSpec((None, C, hw), lambda b, n: (b, nmap(n), 1)),
                pl.BlockSpec((None, C, hw), lambda b, n: (b, nmap(n), 2)),
                pl.BlockSpec((None, C, 128), lambda b, n: (b, nmap(n), 0)),
                pl.BlockSpec((N_TAIL, C), lambda b, n: (0, b * N + nmap(n)))]

    fwd = lambda n: n
    bwd = lambda n: N - 1 - n
    out_sd = jax.ShapeDtypeStruct((B, S, hw), F32)
    return pl.pallas_call(
        _dn_kernel,
        grid=(B, N),
        in_specs=specs(fwd) + specs(bwd),
        out_specs=[pl.BlockSpec((None, C, hw), lambda b, n: (b, fwd(n), 0)),
                   pl.BlockSpec((None, C, hw), lambda b, n: (b, bwd(n), 0))],
        out_shape=[out_sd, out_sd],
        scratch_shapes=[pltpu.VMEM((2, DN_HEADS, DN_DIM, DN_DIM), F32)],
        compiler_params=_cparams(("parallel", "arbitrary")),
        name="deltanet_scan",
    )(qkv, qkv, qkv, bg3, bgT, qkv, qkv, qkv, bg3, bgT)


def _layer_norm(y, g, b):
    mu = jnp.mean(y, axis=-1, keepdims=True)
    var = jnp.mean(jnp.square(y - mu), axis=-1, keepdims=True)
    return (y - mu) * lax.rsqrt(var + LN_EPS) * g + b


def _outproj_kernel(attn_ref, of_ref, ob_ref, z0_ref, z1_ref, x_ref, w_ref, nw_ref, g_ref, b_ref, h_ref, mix_ref):
    aw = ATTN_HEADS * HEAD_DIM
    mix_ref[:, :aw] = attn_ref[...]
    nw = nw_ref[...]
    half = z0_ref.shape[1] // DN_DIM
    for hh in range(DN_HEADS):
        sl = slice(hh * DN_DIM, (hh + 1) * DN_DIM)
        o = of_ref[:, sl] + ob_ref[:, sl]
        z_ref = z0_ref if hh < half else z1_ref
        zc = (hh % half) * DN_DIM
        z = z_ref[:, zc:zc + DN_DIM].astype(F32)
        o = o * lax.rsqrt(jnp.mean(o * o, axis=-1, keepdims=True) + RMS_EPS) * nw
        mix_ref[:, aw + hh * DN_DIM:aw + (hh + 1) * DN_DIM] = (o * jax.nn.silu(z)).astype(BF16)
    y = DEEPNORM_ALPHA * x_ref[...] + _dot(mix_ref[...], w_ref[...])
    h_ref[...] = _layer_norm(y, g_ref[...], b_ref[...])


def _outproj(attn2, of2, ob2, proj2, x2, w_out_b, norm_w, ln_g, ln_b, tm):
    M, D = x2.shape
    dw = DN_HEADS * DN_DIM
    zb = dw // 2
    z0 = COL_DZ // zb
    row = lambda w: pl.BlockSpec((tm, w), lambda i: (i, 0))
    full = lambda shape: pl.BlockSpec(shape, lambda i: (0, 0))
    return pl.pallas_call(
        _outproj_kernel,
        grid=(M // tm,),
        in_specs=[row(ATTN_HEADS * HEAD_DIM), row(dw), row(dw),
                  pl.BlockSpec((tm, zb), lambda i: (i, z0)), pl.BlockSpec((tm, zb), lambda i: (i, z0 + 1)),
                  row(D), full(w_out_b.shape), full((1, DN_DIM)), full((1, D)), full((1, D))],
        out_specs=row(D),
        out_shape=jax.ShapeDtypeStruct((M, D), F32),
        scratch_shapes=[pltpu.VMEM((tm, ATTN_HEADS * HEAD_DIM + dw), BF16)],
        compiler_params=_cparams(("parallel",)),
        name="outproj_ln",
    )(attn2, of2, ob2, proj2, proj2, x2, w_out_b, norm_w.reshape(1, -1), ln_g.reshape(1, -1), ln_b.reshape(1, -1))


TOPK_W = 128


def _topk_rounds(s, val_ref, idx_ref):
    iota = lax.broadcasted_iota(I32, s.shape, 0)
    for r in range(PEER_TOPK):
        m = jnp.max(s, axis=0, keepdims=True)
        idx = jnp.min(jnp.where(s == m, iota, s.shape[0]), axis=0, keepdims=True)
        val_ref[r:r + 1, :] = m
        idx_ref[r:r + 1, :] = idx
        s = jnp.where(iota == idx, -jnp.inf, s)


def _pick_rows(sel, table):
    out = jnp.zeros(sel.shape, table.dtype)
    for a in range(PEER_TOPK):
        out = jnp.where(sel == a, table[a:a + 1, :], out)
    return out


def _peer_topk_kernel(h_ref, wq_ref, keys_ref, i_ref, j_ref, g_ref,
                      qs_ref, sv_ref, si_ref, cand_ref, tv_ref, tc_ref, *, tq):
    hd = pl.program_id(1)
    nq = 2 * PEER_HEADS

    @pl.when(hd == 0)
    def _():
        q = _dot(h_ref[...].astype(BF16), wq_ref[...])
        for c in range(nq):
            qs_ref[c] = q[:, c * PEER_HALF:(c + 1) * PEER_HALF].astype(BF16)

    for slab in range(tq // TOPK_W):
        lanes = slice(slab * TOPK_W, (slab + 1) * TOPK_W)
        for p in range(2):
            qhp = qs_ref[2 * hd + p, lanes, :]
            s = _dot_nt(keys_ref[p], qhp)
            _topk_rounds(s, sv_ref.at[p], si_ref.at[p])
        sv2 = sv_ref[1]
        for a in range(PEER_TOPK):
            cand_ref[a * PEER_TOPK:(a + 1) * PEER_TOPK, :] = sv_ref[0, a:a + 1, :] + sv2
        _topk_rounds(cand_ref[...], tv_ref, tc_ref)
        top_s = tv_ref[...]
        top_c = tc_ref[...]
        e = jnp.exp(top_s - top_s[0:1, :])
        g_ref[:, lanes] = e / jnp.sum(e, axis=0, keepdims=True)
        shift = int(math.log2(PEER_TOPK))
        i_ref[:, lanes] = _pick_rows(top_c >> shift, si_ref[0])
        j_ref[:, lanes] = _pick_rows(top_c & (PEER_TOPK - 1), si_ref[1])


def _peer_topk(h2, wq_b, keys_b, tq):
    M, D = h2.shape
    nq = 2 * PEER_HEADS
    out_spec = pl.BlockSpec((PEER_TOPK, tq), lambda i, hd: (hd, i))
    np_ = PEER_HEADS * PEER_TOPK
    return pl.pallas_call(
        functools.partial(_peer_topk_kernel, tq=tq),
        grid=(M // tq, PEER_HEADS),
        in_specs=[pl.BlockSpec((tq, D), lambda i, hd: (i, 0)),
                  pl.BlockSpec(wq_b.shape, lambda i, hd: (0, 0)),
                  pl.BlockSpec(keys_b.shape, lambda i, hd: (0, 0, 0))],
        out_specs=[out_spec, out_spec, out_spec],
        out_shape=[jax.ShapeDtypeStruct((np_, M), I32), jax.ShapeDtypeStruct((np_, M), I32),
                   jax.ShapeDtypeStruct((np_, M), F32)],
        scratch_shapes=[pltpu.VMEM((nq, tq, PEER_HALF), BF16),
                        pltpu.VMEM((2, PEER_TOPK, TOPK_W), F32), pltpu.VMEM((2, PEER_TOPK, TOPK_W), I32),
                        pltpu.VMEM((PEER_TOPK * PEER_TOPK, TOPK_W), F32),
                        pltpu.VMEM((PEER_TOPK, TOPK_W), F32), pltpu.VMEM((PEER_TOPK, TOPK_W), I32)],
        compiler_params=_cparams(("parallel", "arbitrary")),
        name="peer_topk",
    )(h2, wq_b, keys_b)


def _gate_kernel(i_ref, j_ref, g_ref, o_ref, *, tg):
    sub = lax.broadcasted_iota(I32, (PEER_NKEYS, PEER_HEADS * PEER_TOPK), 0)

    def body(t, carry):
        irow = i_ref[pl.ds(t, 1), :]
        jrow = j_ref[pl.ds(t, 1), :]
        grow = g_ref[pl.ds(t, 1), :]
        at = jnp.where(sub == irow, grow, 0.0).astype(BF16)
        bt = jnp.where(sub == jrow, 1.0, 0.0).astype(BF16)
        o_ref[t] = _dot_nt(at, bt).astype(o_ref.dtype)
        return carry

    lax.fori_loop(0, tg, body, 0, unroll=8)


def _gate_matrix(i_n, j_n, g_n, tg):
    M, P = i_n.shape
    spec = pl.BlockSpec((tg, P), lambda i: (i, 0))
    return pl.pallas_call(
        functools.partial(_gate_kernel, tg=tg),
        grid=(M // tg,),
        in_specs=[spec, spec, spec],
        out_specs=pl.BlockSpec((tg, PEER_NKEYS, PEER_NKEYS), lambda i: (i, 0, 0)),
        out_shape=jax.ShapeDtypeStruct((M, PEER_NKEYS, PEER_NKEYS), F32),
        compiler_params=_cparams(("parallel",)),
        name="peer_gates",
    )(i_n, j_n, g_n)


def _peer_dense_kernel(h_ref, u_ref, v_ref, gm_ref, g_ref, b_ref, o_ref, hb_ref, p_ref):
    j = pl.program_id(1)

    @pl.when(j == 0)
    def _():
        hb_ref[...] = h_ref[...].astype(BF16)
        o_ref[...] = jnp.zeros_like(o_ref)

    s = _dot_nt(hb_ref[...], u_ref[...])
    act = 0.5 * s * (1.0 + lax.erf(s * (2.0 ** -0.5)))
    for r in range(gm_ref.shape[1]):
        sl = slice(r * PEER_NKEYS, (r + 1) * PEER_NKEYS)
        p_ref[:, sl] = (act[:, sl] * gm_ref[:, r, :]).astype(BF16)
    o_ref[...] += _dot(p_ref[...], v_ref[...])

    @pl.when(j == pl.num_programs(1) - 1)
    def _():
        y = DEEPNORM_ALPHA * h_ref[...] + o_ref[...]
        o_ref[...] = _layer_norm(y, g_ref[...], b_ref[...])


def _peer_dense(h2, u_b, v_b, gmat, ln_g, ln_b, tt, eb):
    M, D = h2.shape
    E = u_b.shape[0]
    rows = eb // PEER_NKEYS
    return pl.pallas_call(
        _peer_dense_kernel,
        grid=(M // tt, E // eb),
        in_specs=[pl.BlockSpec((tt, D), lambda i, j: (i, 0)),
                  pl.BlockSpec((eb, D), lambda i, j: (j, 0)),
                  pl.BlockSpec((eb, D), lambda i, j: (j, 0)),
                  pl.BlockSpec((tt, rows, PEER_NKEYS), lambda i, j: (i, j, 0)),
                  pl.BlockSpec((1, D), lambda i, j: (0, 0)),
                  pl.BlockSpec((1, D), lambda i, j: (0, 0))],
        out_specs=pl.BlockSpec((tt, D), lambda i, j: (i, 0)),
        out_shape=jax.ShapeDtypeStruct((M, D), F32),
        scratch_shapes=[pltpu.VMEM((tt, D), BF16), pltpu.VMEM((tt, eb), BF16)],
        compiler_params=_cparams(("parallel", "arbitrary")),
        name="peer_dense",
    )(h2, u_b, v_b, gmat, ln_g.reshape(1, -1), ln_b.reshape(1, -1))


def _tile(n, pref):
    t = min(n, pref)
    assert n % t == 0
    return t


def _layer(x, w_in, conv_w, a_log, dt_bias, dn_norm_w, attn_sink, rel_bias, w_out,
           ln1_g, ln1_b, peer_wq, peer_keys, peer_u, peer_v, ln2_g, ln2_b):
    B, S, D = x.shape
    T = B * S
    x2 = x.reshape(T, D)

    proj = _inproj(x2, w_in[:, :COL_TAIL].astype(BF16), _tile(T, 1024), 512)
    bg, bgT = _tail(x2, w_in[:, COL_TAIL:], a_log, dt_bias, _tile(T, 1024))
    proj3 = proj.reshape(B, S, -1)

    sink_b = jnp.broadcast_to(attn_sink.astype(F32).reshape(ATTN_KV_HEADS, ATTN_GROUP, 1),
                              (ATTN_KV_HEADS, ATTN_GROUP, 128))
    attn = _attention(proj3, _attn_bias_table(rel_bias), sink_b)

    qkv = _conv(proj3, conv_w, _tile(S, 512))
    o_f, o_b = _deltanet(qkv, bg.reshape(B, S, 128), bgT)

    h = _outproj(attn.reshape(T, -1), o_f.reshape(T, -1), o_b.reshape(T, -1), proj, x2,
                 w_out.astype(BF16), dn_norm_w, ln1_g, ln1_b, _tile(T, 512))

    i_t, j_t, g_t = _peer_topk(h, peer_wq.astype(BF16), peer_keys.astype(BF16), _tile(T, 256))
    gmat = _gate_matrix(i_t.T, j_t.T, g_t.T, _tile(T, 64))
    y = _peer_dense(h, peer_u.astype(BF16), peer_v.astype(BF16), gmat, ln2_g, ln2_b, _tile(T, 512), 1024)
    return y.reshape(B, S, D)


def kernel(x, w_in, conv_w, a_log, dt_bias, dn_norm_w, attn_sink, rel_bias, w_out, ln1_g, ln1_b,
           peer_wq, peer_keys, peer_u, peer_v, ln2_g, ln2_b):
    for l in range(DEPTH):
        x = _layer(x, w_in[l], conv_w[l], a_log[l], dt_bias[l], dn_norm_w[l], attn_sink[l], rel_bias,
                   w_out[l], ln1_g[l], ln1_b[l], peer_wq[l], peer_keys[l], peer_u[l], peer_v[l],
                   ln2_g[l], ln2_b[l])
    return x
```

```python
import functools
import math

import jax
import jax.numpy as jnp
import numpy as np
from jax import lax
from jax.experimental import pallas as pl
from jax.experimental.pallas import tpu as pltpu

F32 = jnp.float32
BF16 = jnp.bfloat16
I32 = jnp.int32

HEAD_DIM = 128
ATTN_HEADS = 8
ATTN_KV_HEADS = 2
ATTN_GROUP = ATTN_HEADS // ATTN_KV_HEADS
WINDOW = 128
ATTN_BLOCK = 128
N_BUCKETS = 32
MAX_DISTANCE = 128
DN_HEADS = 8
DN_DIM = 128
CONV_WIDTH = 5
DN_CHUNK = 128
PEER_HEADS = 8
PEER_NKEYS = 128
PEER_HALF = 128
PEER_TOPK = 16
DEPTH = 1
DEEPNORM_ALPHA = (2.0 * DEPTH) ** 0.25
LN_EPS = 1e-5
RMS_EPS = 1e-6
NEG_BIG = -1e30

COL_AQ = 0
COL_AK = COL_AQ + ATTN_HEADS * HEAD_DIM
COL_AV = COL_AK + ATTN_KV_HEADS * HEAD_DIM
COL_DQ = COL_AV + ATTN_KV_HEADS * HEAD_DIM
COL_DZ = COL_DQ + 3 * DN_HEADS * DN_DIM
COL_TAIL = COL_DZ + DN_HEADS * DN_DIM
N_TAIL = 4 * DN_HEADS

VMEM_LIMIT = 56 * 1024 * 1024


def _cparams(sem):
    return pltpu.CompilerParams(dimension_semantics=sem, vmem_limit_bytes=VMEM_LIMIT)


def _dot(a, b):
    return jnp.dot(a, b, preferred_element_type=F32)


def _dot_nt(a, b, precision=None):
    return lax.dot_general(a, b, (((1,), (1,)), ((), ())), preferred_element_type=F32, precision=precision)


def _inproj_kernel(x_ref, w_ref, o_ref, xb_ref):
    @pl.when(pl.program_id(1) == 0)
    def _():
        xb_ref[...] = x_ref[...].astype(BF16)

    o_ref[...] = _dot(xb_ref[...], w_ref[...]).astype(o_ref.dtype)


def _inproj(x2, w_main, tm, tn):
    M, K = x2.shape
    N = w_main.shape[1]
    return pl.pallas_call(
        _inproj_kernel,
        grid=(M // tm, N // tn),
        in_specs=[pl.BlockSpec((tm, K), lambda i, j: (i, 0)),
                  pl.BlockSpec((K, tn), lambda i, j: (0, j))],
        out_specs=pl.BlockSpec((tm, tn), lambda i, j: (i, j)),
        out_shape=jax.ShapeDtypeStruct((M, N), BF16),
        scratch_shapes=[pltpu.VMEM((tm, K), BF16)],
        compiler_params=_cparams(("parallel", "arbitrary")),
        name="inproj",
    )(x2, w_main)


def _tail_kernel(x_ref, wt_ref, wtT_ref, alog_r_ref, dtb_r_ref, alog_c_ref, dtb_c_ref, bg_ref, bgT_ref):
    xb = x_ref[...].astype(BF16)
    t = _dot(xb, wt_ref[...])
    tT = _dot_nt(wtT_ref[...], xb)
    lane = lax.broadcasted_iota(I32, t.shape, 1)
    g = -jnp.exp(alog_r_ref[...]) * jax.nn.softplus(t + dtb_r_ref[...])
    bg_ref[...] = jnp.where(lane < 2 * DN_HEADS, jax.nn.sigmoid(t), g)
    sub = lax.broadcasted_iota(I32, tT.shape, 0)
    gT = -jnp.exp(alog_c_ref[...]) * jax.nn.softplus(tT + dtb_c_ref[...])
    bgT_ref[...] = jnp.where(sub < 2 * DN_HEADS, jax.nn.sigmoid(tT), gT)


def _tail(x2, w_tail, a_log, dt_bias, tm):
    M, K = x2.shape
    wt = jnp.zeros((K, 128), BF16).at[:, :N_TAIL].set(w_tail.astype(BF16))
    wtT = w_tail.astype(BF16).T
    al = jnp.zeros((N_TAIL,), F32).at[2 * DN_HEADS:].set(a_log.reshape(-1))
    db = jnp.zeros((N_TAIL,), F32).at[2 * DN_HEADS:].set(dt_bias.reshape(-1))
    al_r = jnp.zeros((1, 128), F32).at[0, :N_TAIL].set(al)
    db_r = jnp.zeros((1, 128), F32).at[0, :N_TAIL].set(db)
    full = lambda shape: pl.BlockSpec(shape, lambda i: (0, 0))
    return pl.pallas_call(
        _tail_kernel,
        grid=(M // tm,),
        in_specs=[pl.BlockSpec((tm, K), lambda i: (i, 0)), full((K, 128)), full((N_TAIL, K)),
                  full((1, 128)), full((1, 128)), full((N_TAIL, 1)), full((N_TAIL, 1))],
        out_specs=[pl.BlockSpec((tm, 128), lambda i: (i, 0)), pl.BlockSpec((N_TAIL, tm), lambda i: (0, i))],
        out_shape=[jax.ShapeDtypeStruct((M, 128), F32), jax.ShapeDtypeStruct((N_TAIL, M), F32)],
        compiler_params=_cparams(("parallel",)),
        name="tail",
    )(x2, wt, wtT, al_r, db_r, al.reshape(N_TAIL, 1), db.reshape(N_TAIL, 1))


def _t5_bucket(rel):
    nb = N_BUCKETS // 2
    max_exact = nb // 2
    ret = jnp.where(rel > 0, nb, 0)
    n = jnp.abs(rel)
    large = max_exact + (jnp.log(jnp.maximum(n, 1).astype(F32) / max_exact)
                         / math.log(MAX_DISTANCE / max_exact) * (nb - max_exact)).astype(I32)
    large = jnp.minimum(large, nb - 1)
    return ret + jnp.where(n < max_exact, n, large)


def _attn_bias_table(rel_bias):
    qi = jnp.arange(ATTN_BLOCK)[:, None]
    kj = jnp.arange(3 * ATTN_BLOCK)[None, :]
    rel = jnp.stack([kj - qi - variant * ATTN_BLOCK for variant in range(3)])
    bucket = jnp.where(jnp.abs(rel) <= WINDOW, _t5_bucket(rel), -1)[:, None]
    tab = jnp.full((3, ATTN_HEADS, ATTN_BLOCK, 3 * ATTN_BLOCK), NEG_BIG, F32)
    rb = rel_bias.astype(F32)
    for bk in range(N_BUCKETS):
        tab = jnp.where(bucket == bk, rb[bk][None, :, None, None], tab)
    return tab


def _attn_kernel(q_ref, k_ref, v_ref, bias_ref, sink_ref, o_ref, *, seq):
    n = pl.program_id(2)
    start = jnp.clip((n - 1) * ATTN_BLOCK, 0, seq - 3 * ATTN_BLOCK)
    start = pl.multiple_of(start, ATTN_BLOCK)
    kb = k_ref[pl.ds(start, 3 * ATTN_BLOCK), :]
    vb = v_ref[pl.ds(start, 3 * ATTN_BLOCK), :]
    for hh in range(ATTN_GROUP):
        q = q_ref[:, hh * HEAD_DIM:(hh + 1) * HEAD_DIM]
        s = _dot_nt(q, kb) * (HEAD_DIM ** -0.5) + bias_ref[hh]
        sk = sink_ref[hh:hh + 1, 0:1]
        m = jnp.maximum(jnp.max(s, axis=-1, keepdims=True), sk)
        p = jnp.exp(s - m)
        den = jnp.sum(p, axis=-1, keepdims=True) + jnp.exp(sk - m)
        o = _dot((p / den).astype(BF16), vb)
        o_ref[:, hh * HEAD_DIM:(hh + 1) * HEAD_DIM] = o.astype(o_ref.dtype)


def _attention(proj3, bias_tab, sink_b):
    B, S, _ = proj3.shape
    nb = S // ATTN_BLOCK
    assert nb >= 3
    gw = ATTN_GROUP * HEAD_DIM
    kcol = COL_AK // HEAD_DIM
    vcol = COL_AV // HEAD_DIM
    variant = lambda n: jnp.where(n == 0, 0, jnp.where(n == nb - 1, 2, 1))
    return pl.pallas_call(
        functools.partial(_attn_kernel, seq=S),
        grid=(B, ATTN_KV_HEADS, nb),
        in_specs=[pl.BlockSpec((None, ATTN_BLOCK, gw), lambda b, g, n: (b, n, g)),
                  pl.BlockSpec((None, S, HEAD_DIM), lambda b, g, n: (b, 0, kcol + g)),
                  pl.BlockSpec((None, S, HEAD_DIM), lambda b, g, n: (b, 0, vcol + g)),
                  pl.BlockSpec((None, ATTN_GROUP, ATTN_BLOCK, 3 * ATTN_BLOCK),
                               lambda b, g, n: (variant(n), g, 0, 0)),
                  pl.BlockSpec((None, ATTN_GROUP, 128), lambda b, g, n: (g, 0, 0))],
        out_specs=pl.BlockSpec((None, ATTN_BLOCK, gw), lambda b, g, n: (b, n, g)),
        out_shape=jax.ShapeDtypeStruct((B, S, ATTN_HEADS * HEAD_DIM), BF16),
        compiler_params=_cparams(("parallel", "parallel", "arbitrary")),
        name="window_attn",
    )(proj3, proj3, proj3, bias_tab, sink_b)


CONV_COLS = 4 * DN_DIM
CONV_HALO = 16


def _conv_kernel(x_ref, prev_ref, next_ref, w_ref, o_ref, buf_ref, *, ts, n_sblk):
    s = pl.program_id(1)
    c = pl.program_id(2)
    pad = CONV_WIDTH // 2
    prev = jnp.where(s > 0, prev_ref[...].astype(F32), 0.0)
    nxt = jnp.where(s < n_sblk - 1, next_ref[...].astype(F32), 0.0)
    buf_ref[0:CONV_HALO, :] = prev
    buf_ref[CONV_HALO:CONV_HALO + ts, :] = x_ref[...].astype(F32)
    buf_ref[CONV_HALO + ts:CONV_HALO + ts + CONV_HALO, :] = nxt
    y = jnp.zeros((ts, CONV_COLS), F32)
    for w in range(CONV_WIDTH):
        y = y + buf_ref[CONV_HALO + w - pad:CONV_HALO + w - pad + ts, :] * w_ref[w:w + 1, :]
    y = jax.nn.silu(y)
    is_qk = c < 4
    qscale = jnp.where(c < 2, DN_DIM ** -0.5, 1.0).astype(F32)
    for hh in range(CONV_COLS // DN_DIM):
        yh = y[:, hh * DN_DIM:(hh + 1) * DN_DIM]
        nrm = yh * lax.rsqrt(jnp.sum(yh * yh, axis=-1, keepdims=True) + RMS_EPS) * qscale
        o_ref[:, hh * DN_DIM:(hh + 1) * DN_DIM] = jnp.where(is_qk, nrm, yh)


def _conv(proj3, conv_w, ts):
    B, S, _ = proj3.shape
    n_sblk = S // ts
    c0 = COL_DQ // CONV_COLS
    ncb = 3 * DN_HEADS * DN_DIM // CONV_COLS
    hb = ts // CONV_HALO
    nh = S // CONV_HALO
    return pl.pallas_call(
        functools.partial(_conv_kernel, ts=ts, n_sblk=n_sblk),
        grid=(B, n_sblk, ncb),
        in_specs=[pl.BlockSpec((None, ts, CONV_COLS), lambda b, s, c: (b, s, c0 + c)),
                  pl.BlockSpec((None, CONV_HALO, CONV_COLS),
                               lambda b, s, c: (b, jnp.maximum(s * hb - 1, 0), c0 + c)),
                  pl.BlockSpec((None, CONV_HALO, CONV_COLS),
                               lambda b, s, c: (b, jnp.minimum((s + 1) * hb, nh - 1), c0 + c)),
                  pl.BlockSpec((CONV_WIDTH, CONV_COLS), lambda b, s, c: (0, c))],
        out_specs=pl.BlockSpec((None, ts, CONV_COLS), lambda b, s, c: (b, s, c)),
        out_shape=jax.ShapeDtypeStruct((B, S, 3 * DN_HEADS * DN_DIM), F32),
        scratch_shapes=[pltpu.VMEM((ts + 2 * CONV_HALO, CONV_COLS), F32)],
        compiler_params=_cparams(("parallel", "parallel", "arbitrary")),
        name="short_conv",
    )(proj3, proj3, proj3, conv_w)


def _dn_direction(d, q_ref, k_ref, v_ref, bg_ref, bgT_ref, o_ref, state_ref):
    C = DN_CHUNK
    hi = lax.Precision.HIGHEST
    row = lax.broadcasted_iota(I32, (C, C), 0)
    col = lax.broadcasted_iota(I32, (C, C), 1)
    if d == 0:
        incl, strict, last, lo, up = row >= col, row > col, C - 1, col, row
    else:
        incl, strict, last, lo, up = row <= col, row < col, 0, row, col
    tri = incl.astype(F32)
    eye = jnp.where(row == col, 1.0, 0.0)
    sels = []
    half = 1
    while half < C:
        sels.append(((row ^ col) < 2 * half) & ((up & half) != 0) & ((lo & half) == 0))
        half *= 2
    bg = bg_ref[...]
    g_cum = jnp.dot(tri, bg, preferred_element_type=F32, precision=hi)
    g_cum_t = _dot_nt(bgT_ref[...], tri, precision=hi)
    H = range(DN_HEADS)
    hs = [slice(h * DN_DIM, (h + 1) * DN_DIM) for h in H]
    beta_c = [bg[:, d * DN_HEADS + h:d * DN_HEADS + h + 1] for h in H]
    g_c = [g_cum[:, 2 * DN_HEADS + d * DN_HEADS + h:2 * DN_HEADS + d * DN_HEADS + h + 1] for h in H]
    g_r = [g_cum_t[2 * DN_HEADS + d * DN_HEADS + h:2 * DN_HEADS + d * DN_HEADS + h + 1, :] for h in H]
    decay = [jnp.where(incl, jnp.exp(jnp.where(incl, g_c[h] - g_r[h], 0.0)), 0.0) for h in H]
    kbeta = [k_ref[:, hs[h]] * beta_c[h] for h in H]
    lmat = [jnp.where(strict, _dot_nt(kbeta[h].astype(BF16), k_ref[:, hs[h]].astype(BF16)) * decay[h], 0.0)
            for h in H]
    x = [eye - jnp.where(sels[0], lmat[h], 0.0) for h in H]
    for sel in sels[1:]:
        xb = [x[h].astype(BF16) for h in H]
        t = [_dot(xb[h], jnp.where(sel, lmat[h], 0.0).astype(BF16)) for h in H]
        x = [x[h] - _dot(t[h].astype(BF16), xb[h]) for h in H]
    e_g = [jnp.exp(g_c[h]) for h in H]
    uw = [_dot(x[h].astype(BF16),
               jnp.concatenate([v_ref[:, hs[h]] * beta_c[h], kbeta[h] * e_g[h]], axis=1).astype(BF16)) for h in H]
    amat = [(_dot_nt(q_ref[:, hs[h]].astype(BF16), k_ref[:, hs[h]].astype(BF16)) * decay[h]).astype(BF16) for h in H]
    g_last = [g_c[h][last:last + 1, :] for h in H]
    state_b = [state_ref[d, h].astype(BF16) for h in H]
    v_new = [(uw[h][:, :DN_DIM] - _dot(uw[h][:, DN_DIM:].astype(BF16), state_b[h])).astype(BF16) for h in H]
    for h in H:
        qg = (q_ref[:, hs[h]] * e_g[h]).astype(BF16)
        o_ref[:, hs[h]] = _dot(qg, state_b[h]) + _dot(amat[h], v_new[h])
    for h in H:
        kd = k_ref[:, hs[h]] * jnp.exp(g_last[h] - g_c[h])
        state_ref[d, h] = state_ref[d, h] * jnp.exp(g_last[h]) + _dot(kd.T.astype(BF16), v_new[h])


def _dn_kernel(qf, kf, vf, bgf, bgTf, qb, kb, vb, bgb, bgTb, of_ref, ob_ref, state_ref):
    @pl.when(pl.program_id(1) == 0)
    def _():
        state_ref[...] = jnp.zeros_like(state_ref)

    _dn_direction(0, qf, kf, vf, bgf, bgTf, of_ref, state_ref)
    _dn_direction(1, qb, kb, vb, bgb, bgTb, ob_ref, state_ref)


def _deltanet(qkv, bg3, bgT):
    B, S, _ = qkv.shape
    C = DN_CHUNK
    N = S // C
    hw = DN_HEADS * DN_DIM

    def specs(nmap):
        return [pl.BlockSpec((None, C, hw), lambda b, n: (b, nmap(n), 0)),
                pl.BlockSpec((None, C, hw), lambda b, n: (b, nmap(n), 1)),
                pl.BlockSpec((None, C, hw), lambda b, n: (b, nmap(n), 2)),
                pl.BlockSpec((None, C, 128), lambda b, n: (b, nmap(n), 0)),
                pl.BlockSpec((N_TAIL, C), lambda b, n: (0, b * N + nmap(n)))]

    fwd = lambda n: n
    bwd = lambda n: N - 1 - n
    out_sd = jax.ShapeDtypeStruct((B, S, hw), F32)
    return pl.pallas_call(
        _dn_kernel,
        grid=(B, N),
        in_specs=specs(fwd) + specs(bwd),
        out_specs=[pl.BlockSpec((None, C, hw), lambda b, n: (b, fwd(n), 0)),
                   pl.BlockSpec((None, C, hw), lambda b, n: (b, bwd(n), 0))],
        out_shape=[out_sd, out_sd],
        scratch_shapes=[pltpu.VMEM((2, DN_HEADS, DN_DIM, DN_DIM), F32)],
        compiler_params=_cparams(("parallel", "arbitrary")),
        name="deltanet_scan",
    )(qkv, qkv, qkv, bg3, bgT, qkv, qkv, qkv, bg3, bgT)


def _layer_norm(y, g, b):
    mu = jnp.mean(y, axis=-1, keepdims=True)
    var = jnp.mean(jnp.square(y - mu), axis=-1, keepdims=True)
    return (y - mu) * lax.rsqrt(var + LN_EPS) * g + b


def _outproj_kernel(attn_ref, of_ref, ob_ref, z0_ref, z1_ref, x_ref, w_ref, nw_ref, g_ref, b_ref, h_ref, mix_ref):
    aw = ATTN_HEADS * HEAD_DIM
    mix_ref[:, :aw] = attn_ref[...]
    nw = nw_ref[...]
    half = z0_ref.shape[1] // DN_DIM
    for hh in range(DN_HEADS):
        sl = slice(hh * DN_DIM, (hh + 1) * DN_DIM)
        o = of_ref[:, sl] + ob_ref[:, sl]
        z_ref = z0_ref if hh < half else z1_ref
        zc = (hh % half) * DN_DIM
        z = z_ref[:, zc:zc + DN_DIM].astype(F32)
        o = o * lax.rsqrt(jnp.mean(o * o, axis=-1, keepdims=True) + RMS_EPS) * nw
        mix_ref[:, aw + hh * DN_DIM:aw + (hh + 1) * DN_DIM] = (o * jax.nn.silu(z)).astype(BF16)
    y = DEEPNORM_ALPHA * x_ref[...] + _dot(mix_ref[...], w_ref[...])
    h_ref[...] = _layer_norm(y, g_ref[...], b_ref[...])


def _outproj(attn2, of2, ob2, proj2, x2, w_out_b, norm_w, ln_g, ln_b, tm):
    M, D = x2.shape
    dw = DN_HEADS * DN_DIM
    zb = dw // 2
    z0 = COL_DZ // zb
    row = lambda w: pl.BlockSpec((tm, w), lambda i: (i, 0))
    full = lambda shape: pl.BlockSpec(shape, lambda i: (0, 0))
    return pl.pallas_call(
        _outproj_kernel,
        grid=(M // tm,),
        in_specs=[row(ATTN_HEADS * HEAD_DIM), row(dw), row(dw),
                  pl.BlockSpec((tm, zb), lambda i: (i, z0)), pl.BlockSpec((tm, zb), lambda i: (i, z0 + 1)),
                  row(D), full(w_out_b.shape), full((1, DN_DIM)), full((1, D)), full((1, D))],
        out_specs=row(D),
        out_shape=jax.ShapeDtypeStruct((M, D), F32),
        scratch_shapes=[pltpu.VMEM((tm, ATTN_HEADS * HEAD_DIM + dw), BF16)],
        compiler_params=_cparams(("parallel",)),
        name="outproj_ln",
    )(attn2, of2, ob2, proj2, proj2, x2, w_out_b, norm_w.reshape(1, -1), ln_g.reshape(1, -1), ln_b.reshape(1, -1))


TOPK_W = 128


def _topk_rounds(ss, labels, val_refs, idx_refs):
    big = jnp.iinfo(jnp.int32).max
    for r in range(PEER_TOPK):
        ms = [jnp.max(s, axis=0, keepdims=True) for s in ss]
        idxs = [jnp.min(jnp.where(s == m, lab, big), axis=0, keepdims=True) for s, m, lab in zip(ss, ms, labels)]
        for n in range(len(ss)):
            val_refs[n][r:r + 1, :] = ms[n]
            idx_refs[n][r:r + 1, :] = idxs[n]
        ss = [jnp.where(lab == idx, -jnp.inf, s) for s, idx, lab in zip(ss, idxs, labels)]


def _staircase_candidates(sv1, sv2):
    w = sv1.shape[1]
    iota8 = lax.broadcasted_iota(I32, (8, w), 0)
    vals, labs = [], []
    for a in range(8):
        nb = PEER_TOPK // (a + 1)
        for b0 in range(0, nb, 8):
            v = sv1[a:a + 1, :] + sv2[b0:b0 + 8, :]
            if nb - b0 < 8:
                v = jnp.where(iota8 < nb - b0, v, -jnp.inf)
            vals.append(v)
            labs.append(iota8 + (a * PEER_TOPK + b0))
    vals.append(sv1[8:16, :] + sv2[0:1, :])
    labs.append((iota8 + 8) * PEER_TOPK)
    return jnp.concatenate(vals, axis=0), jnp.concatenate(labs, axis=0)


def _pick_rows(sel, table):
    out = jnp.zeros(sel.shape, table.dtype)
    for a in range(PEER_TOPK):
        out = jnp.where(sel == a, table[a:a + 1, :], out)
    return out


def _peer_topk_kernel(h_ref, wq_ref, keys_ref, i_ref, j_ref, g_ref,
                      qs_ref, sv_ref, si_ref, tv_ref, tc_ref, *, tq):
    hd = pl.program_id(1)
    nq = 2 * PEER_HEADS

    @pl.when(hd == 0)
    def _():
        q = _dot(h_ref[...].astype(BF16), wq_ref[...])
        for c in range(nq):
            qs_ref[c] = q[:, c * PEER_HALF:(c + 1) * PEER_HALF].astype(BF16)

    slabs = range(tq // TOPK_W)
    lanes = [slice(sl * TOPK_W, (sl + 1) * TOPK_W) for sl in slabs]
    key_iota = lax.broadcasted_iota(I32, (PEER_NKEYS, TOPK_W), 0)
    probs = [(sl, p) for sl in slabs for p in range(2)]
    scores = [_dot_nt(keys_ref[p], qs_ref[2 * hd + p, lanes[sl], :]) for sl, p in probs]
    _topk_rounds(scores, [key_iota] * len(probs),
                 [sv_ref.at[sl, p] for sl, p in probs], [si_ref.at[sl, p] for sl, p in probs])
    cands = [_staircase_candidates(sv_ref[sl, 0], sv_ref[sl, 1]) for sl in slabs]
    _topk_rounds([c[0] for c in cands], [c[1] for c in cands],
                 [tv_ref.at[sl] for sl in slabs], [tc_ref.at[sl] for sl in slabs])
    shift = int(math.log2(PEER_TOPK))
    for sl in slabs:
        top_s = tv_ref[sl]
        top_c = tc_ref[sl]
        e = jnp.exp(top_s - top_s[0:1, :])
        g_ref[:, lanes[sl]] = e / jnp.sum(e, axis=0, keepdims=True)
        i_ref[:, lanes[sl]] = _pick_rows(top_c >> shift, si_ref[sl, 0])
        j_ref[:, lanes[sl]] = _pick_rows(top_c & (PEER_TOPK - 1), si_ref[sl, 1])


def _peer_topk(h2, wq_b, keys_b, tq):
    M, D = h2.shape
    nq = 2 * PEER_HEADS
    out_spec = pl.BlockSpec((PEER_TOPK, tq), lambda i, hd: (hd, i))
    np_ = PEER_HEADS * PEER_TOPK
    return pl.pallas_call(
        functools.partial(_peer_topk_kernel, tq=tq),
        grid=(M // tq, PEER_HEADS),
        in_specs=[pl.BlockSpec((tq, D), lambda i, hd: (i, 0)),
                  pl.BlockSpec(wq_b.shape, lambda i, hd: (0, 0)),
                  pl.BlockSpec(keys_b.shape, lambda i, hd: (0, 0, 0))],
        out_specs=[out_spec, out_spec, out_spec],
        out_shape=[jax.ShapeDtypeStruct((np_, M), I32), jax.ShapeDtypeStruct((np_, M), I32),
                   jax.ShapeDtypeStruct((np_, M), F32)],
        scratch_shapes=[pltpu.VMEM((nq, tq, PEER_HALF), BF16),
                        pltpu.VMEM((tq // TOPK_W, 2, PEER_TOPK, TOPK_W), F32),
                        pltpu.VMEM((tq // TOPK_W, 2, PEER_TOPK, TOPK_W), I32),
                        pltpu.VMEM((tq // TOPK_W, PEER_TOPK, TOPK_W), F32),
                        pltpu.VMEM((tq // TOPK_W, PEER_TOPK, TOPK_W), I32)],
        compiler_params=_cparams(("parallel", "arbitrary")),
        name="peer_topk",
    )(h2, wq_b, keys_b)


def _gate_kernel(i_ref, j_ref, g_ref, o_ref, *, tg):
    sub = lax.broadcasted_iota(I32, (PEER_NKEYS, PEER_HEADS * PEER_TOPK), 0)
    rows = o_ref.shape[2]

    def body(t, carry):
        irow = i_ref[pl.ds(t, 1), :]
        jrow = j_ref[pl.ds(t, 1), :]
        grow = g_ref[pl.ds(t, 1), :]
        at = jnp.where(sub == irow, grow, 0.0).astype(BF16)
        bt = jnp.where(sub == jrow, 1.0, 0.0).astype(BF16)
        gt = _dot_nt(at, bt)
        for blk in range(o_ref.shape[0]):
            o_ref[blk, t] = gt[blk * rows:(blk + 1) * rows, :]
        return carry

    lax.fori_loop(0, tg, body, 0, unroll=8)


def _gate_matrix(i_n, j_n, g_n, tg, tt, rows):
    M, P = i_n.shape
    nblk = PEER_NKEYS // rows
    per_tile = tt // tg
    spec = pl.BlockSpec((tg, P), lambda i: (i, 0))
    return pl.pallas_call(
        functools.partial(_gate_kernel, tg=tg),
        grid=(M // tg,),
        in_specs=[spec, spec, spec],
        out_specs=pl.BlockSpec((None, nblk, tg, rows, PEER_NKEYS),
                               lambda i: (i // per_tile, 0, i % per_tile, 0, 0)),
        out_shape=jax.ShapeDtypeStruct((M // tt, nblk, tt, rows, PEER_NKEYS), F32),
        compiler_params=_cparams(("parallel",)),
        name="peer_gates",
    )(i_n, j_n, g_n)


def _peer_dense_kernel(h_ref, u_ref, v_ref, gm_ref, g_ref, b_ref, o_ref, hb_ref, p_ref):
    j = pl.program_id(1)

    @pl.when(j == 0)
    def _():
        hb_ref[...] = h_ref[...].astype(BF16)
        o_ref[...] = jnp.zeros_like(o_ref)

    s = _dot_nt(hb_ref[...], u_ref[...])
    act = 0.5 * s * (1.0 + lax.erf(s * (2.0 ** -0.5)))
    tt = s.shape[0]
    rows = gm_ref.shape[0] // tt
    for r in range(rows):
        sl = slice(r * PEER_NKEYS, (r + 1) * PEER_NKEYS)
        p_ref[:, sl] = (act[:, sl] * gm_ref[pl.ds(r, tt, stride=rows), :]).astype(BF16)
    o_ref[...] += _dot(p_ref[...], v_ref[...])

    @pl.when(j == pl.num_programs(1) - 1)
    def _():
        y = DEEPNORM_ALPHA * h_ref[...] + o_ref[...]
        o_ref[...] = _layer_norm(y, g_ref[...], b_ref[...])


def _peer_dense(h2, u_b, v_b, gmat, ln_g, ln_b, tt, eb):
    M, D = h2.shape
    E = u_b.shape[0]
    rows = eb // PEER_NKEYS
    return pl.pallas_call(
        _peer_dense_kernel,
        grid=(M // tt, E // eb),
        in_specs=[pl.BlockSpec((tt, D), lambda i, j: (i, 0)),
                  pl.BlockSpec((eb, D), lambda i, j: (j, 0)),
                  pl.BlockSpec((eb, D), lambda i, j: (j, 0)),
                  pl.BlockSpec((None, None, tt * rows, PEER_NKEYS), lambda i, j: (i, j, 0, 0)),
                  pl.BlockSpec((1, D), lambda i, j: (0, 0)),
                  pl.BlockSpec((1, D), lambda i, j: (0, 0))],
        out_specs=pl.BlockSpec((tt, D), lambda i, j: (i, 0)),
        out_shape=jax.ShapeDtypeStruct((M, D), F32),
        scratch_shapes=[pltpu.VMEM((tt, D), BF16), pltpu.VMEM((tt, eb), BF16)],
        compiler_params=_cparams(("parallel", "arbitrary")),
        name="peer_dense",
    )(h2, u_b, v_b, gmat, ln_g.reshape(1, -1), ln_b.reshape(1, -1))


def _tile(n, pref):
    t = min(n, pref)
    assert n % t == 0
    return t


def _layer(x, w_in, conv_w, a_log, dt_bias, dn_norm_w, attn_sink, rel_bias, w_out,
           ln1_g, ln1_b, peer_wq, peer_keys, peer_u, peer_v, ln2_g, ln2_b):
    B, S, D = x.shape
    T = B * S
    x2 = x.reshape(T, D)

    proj = _inproj(x2, w_in[:, :COL_TAIL].astype(BF16), _tile(T, 1024), 512)
    bg, bgT = _tail(x2, w_in[:, COL_TAIL:], a_log, dt_bias, _tile(T, 1024))
    proj3 = proj.reshape(B, S, -1)

    sink_b = jnp.broadcast_to(attn_sink.astype(F32).reshape(ATTN_KV_HEADS, ATTN_GROUP, 1),
                              (ATTN_KV_HEADS, ATTN_GROUP, 128))
    attn = _attention(proj3, _attn_bias_table(rel_bias), sink_b)

    qkv = _conv(proj3, conv_w, _tile(S, 512))
    o_f, o_b = _deltanet(qkv, bg.reshape(B, S, 128), bgT)

    h = _outproj(attn.reshape(T, -1), o_f.reshape(T, -1), o_b.reshape(T, -1), proj, x2,
                 w_out.astype(BF16), dn_norm_w, ln1_g, ln1_b, _tile(T, 512))

    i_t, j_t, g_t = _peer_topk(h, peer_wq.astype(BF16), peer_keys.astype(BF16), _tile(T, 256))
    tt, eb = _tile(T, 512), 1024
    rows = eb // PEER_NKEYS
    gmat = _gate_matrix(i_t.T, j_t.T, g_t.T, _tile(T, 64), tt, rows)
    gmat = gmat.reshape(T // tt, PEER_NKEYS // rows, tt * rows, PEER_NKEYS)
    y = _peer_dense(h, peer_u.astype(BF16), peer_v.astype(BF16), gmat, ln2_g, ln2_b, tt, eb)
    return y.reshape(B, S, D)


def kernel(x, w_in, conv_w, a_log, dt_bias, dn_norm_w, attn_sink, rel_bias, w_out, ln1_g, ln1_b,
           peer_wq, peer_keys, peer_u, peer_v, ln2_g, ln2_b):
    for l in range(DEPTH):
        x = _layer(x, w_in[l], conv_w[l], a_log[l], dt_bias[l], dn_norm_w[l], attn_sink[l], rel_bias,
                   w_out[l], ln1_g[l], ln1_b[l], peer_wq[l], peer_keys[l], peer_u[l], peer_v[l],
                   ln2_g[l], ln2_b[l])
    return x
```

```python
import functools
import math

import jax
import jax.numpy as jnp
import numpy as np
from jax import lax
from jax.experimental import pallas as pl
from jax.experimental.pallas import tpu as pltpu

F32 = jnp.float32
BF16 = jnp.bfloat16
I32 = jnp.int32

HEAD_DIM = 128
ATTN_HEADS = 8
ATTN_KV_HEADS = 2
ATTN_GROUP = ATTN_HEADS // ATTN_KV_HEADS
WINDOW = 128
ATTN_BLOCK = 128
N_BUCKETS = 32
MAX_DISTANCE = 128
DN_HEADS = 8
DN_DIM = 128
CONV_WIDTH = 5
DN_CHUNK = 128
PEER_HEADS = 8
PEER_NKEYS = 128
PEER_HALF = 128
PEER_TOPK = 16
DEPTH = 1
DEEPNORM_ALPHA = (2.0 * DEPTH) ** 0.25
LN_EPS = 1e-5
RMS_EPS = 1e-6
NEG_BIG = -1e30

COL_AQ = 0
COL_AK = COL_AQ + ATTN_HEADS * HEAD_DIM
COL_AV = COL_AK + ATTN_KV_HEADS * HEAD_DIM
COL_DQ = COL_AV + ATTN_KV_HEADS * HEAD_DIM
COL_DZ = COL_DQ + 3 * DN_HEADS * DN_DIM
COL_TAIL = COL_DZ + DN_HEADS * DN_DIM
N_TAIL = 4 * DN_HEADS

VMEM_LIMIT = 56 * 1024 * 1024


def _cparams(sem):
    return pltpu.CompilerParams(dimension_semantics=sem, vmem_limit_bytes=VMEM_LIMIT)


def _dot(a, b):
    return jnp.dot(a, b, preferred_element_type=F32)


def _dot_nt(a, b, precision=None):
    return lax.dot_general(a, b, (((1,), (1,)), ((), ())), preferred_element_type=F32, precision=precision)


def _inproj_kernel(x_ref, w_ref, o_ref, xb_ref):
    @pl.when(pl.program_id(1) == 0)
    def _():
        xb_ref[...] = x_ref[...].astype(BF16)

    o_ref[...] = _dot(xb_ref[...], w_ref[...]).astype(o_ref.dtype)


def _inproj(x2, w_main, tm, tn):
    M, K = x2.shape
    N = w_main.shape[1]
    return pl.pallas_call(
        _inproj_kernel,
        grid=(M // tm, N // tn),
        in_specs=[pl.BlockSpec((tm, K), lambda i, j: (i, 0)),
                  pl.BlockSpec((K, tn), lambda i, j: (0, j))],
        out_specs=pl.BlockSpec((tm, tn), lambda i, j: (i, j)),
        out_shape=jax.ShapeDtypeStruct((M, N), BF16),
        scratch_shapes=[pltpu.VMEM((tm, K), BF16)],
        compiler_params=_cparams(("parallel", "arbitrary")),
        name="inproj",
    )(x2, w_main)


def _tail_kernel(x_ref, wt_ref, wtT_ref, alog_r_ref, dtb_r_ref, alog_c_ref, dtb_c_ref, bg_ref, bgT_ref):
    xb = x_ref[...].astype(BF16)
    t = _dot(xb, wt_ref[...])
    tT = _dot_nt(wtT_ref[...], xb)
    lane = lax.broadcasted_iota(I32, t.shape, 1)
    g = -jnp.exp(alog_r_ref[...]) * jax.nn.softplus(t + dtb_r_ref[...])
    bg_ref[...] = jnp.where(lane < 2 * DN_HEADS, jax.nn.sigmoid(t), g)
    sub = lax.broadcasted_iota(I32, tT.shape, 0)
    gT = -jnp.exp(alog_c_ref[...]) * jax.nn.softplus(tT + dtb_c_ref[...])
    bgT_ref[...] = jnp.where(sub < 2 * DN_HEADS, jax.nn.sigmoid(tT), gT)


def _tail(x2, w_tail, a_log, dt_bias, tm):
    M, K = x2.shape
    wt = jnp.zeros((K, 128), BF16).at[:, :N_TAIL].set(w_tail.astype(BF16))
    wtT = w_tail.astype(BF16).T
    al = jnp.zeros((N_TAIL,), F32).at[2 * DN_HEADS:].set(a_log.reshape(-1))
    db = jnp.zeros((N_TAIL,), F32).at[2 * DN_HEADS:].set(dt_bias.reshape(-1))
    al_r = jnp.zeros((1, 128), F32).at[0, :N_TAIL].set(al)
    db_r = jnp.zeros((1, 128), F32).at[0, :N_TAIL].set(db)
    full = lambda shape: pl.BlockSpec(shape, lambda i: (0, 0))
    return pl.pallas_call(
        _tail_kernel,
        grid=(M // tm,),
        in_specs=[pl.BlockSpec((tm, K), lambda i: (i, 0)), full((K, 128)), full((N_TAIL, K)),
                  full((1, 128)), full((1, 128)), full((N_TAIL, 1)), full((N_TAIL, 1))],
        out_specs=[pl.BlockSpec((tm, 128), lambda i: (i, 0)), pl.BlockSpec((N_TAIL, tm), lambda i: (0, i))],
        out_shape=[jax.ShapeDtypeStruct((M, 128), F32), jax.ShapeDtypeStruct((N_TAIL, M), F32)],
        compiler_params=_cparams(("parallel",)),
        name="tail",
    )(x2, wt, wtT, al_r, db_r, al.reshape(N_TAIL, 1), db.reshape(N_TAIL, 1))


def _t5_bucket(rel):
    nb = N_BUCKETS // 2
    max_exact = nb // 2
    ret = jnp.where(rel > 0, nb, 0)
    n = jnp.abs(rel)
    large = max_exact + (jnp.log(jnp.maximum(n, 1).astype(F32) / max_exact)
                         / math.log(MAX_DISTANCE / max_exact) * (nb - max_exact)).astype(I32)
    large = jnp.minimum(large, nb - 1)
    return ret + jnp.where(n < max_exact, n, large)


def _attn_bias_table(rel_bias):
    qi = jnp.arange(ATTN_BLOCK)[:, None]
    kj = jnp.arange(3 * ATTN_BLOCK)[None, :]
    rel = jnp.stack([kj - qi - variant * ATTN_BLOCK for variant in range(3)])
    bucket = jnp.where(jnp.abs(rel) <= WINDOW, _t5_bucket(rel), -1)[:, None]
    tab = jnp.full((3, ATTN_HEADS, ATTN_BLOCK, 3 * ATTN_BLOCK), NEG_BIG, F32)
    rb = rel_bias.astype(F32)
    for bk in range(N_BUCKETS):
        tab = jnp.where(bucket == bk, rb[bk][None, :, None, None], tab)
    return tab


def _attn_kernel(q_ref, k_ref, v_ref, bias_ref, sink_ref, o_ref, *, seq):
    n = pl.program_id(1)
    start = jnp.clip((n - 1) * ATTN_BLOCK, 0, seq - 3 * ATTN_BLOCK)
    start = pl.multiple_of(start, ATTN_BLOCK)
    H = range(ATTN_HEADS)
    hs = [slice(h * HEAD_DIM, (h + 1) * HEAD_DIM) for h in H]
    kvs = [slice(g * HEAD_DIM, (g + 1) * HEAD_DIM) for g in range(ATTN_KV_HEADS)]
    kb = [k_ref[pl.ds(start, 3 * ATTN_BLOCK), sl] for sl in kvs]
    vb = [v_ref[pl.ds(start, 3 * ATTN_BLOCK), sl] for sl in kvs]
    s = [_dot_nt(q_ref[:, hs[h]], kb[h // ATTN_GROUP]) * (HEAD_DIM ** -0.5) + bias_ref[h] for h in H]
    sk = [sink_ref[h // ATTN_GROUP, h % ATTN_GROUP:h % ATTN_GROUP + 1, 0:1] for h in H]
    m = [jnp.maximum(jnp.max(s[h], axis=-1, keepdims=True), sk[h]) for h in H]
    p = [jnp.exp(s[h] - m[h]) for h in H]
    den = [jnp.sum(p[h], axis=-1, keepdims=True) + jnp.exp(sk[h] - m[h]) for h in H]
    o = [_dot((p[h] / den[h]).astype(BF16), vb[h // ATTN_GROUP]) for h in H]
    for h in H:
        o_ref[:, hs[h]] = o[h].astype(o_ref.dtype)


def _attention(proj3, bias_tab, sink_b):
    B, S, _ = proj3.shape
    nb = S // ATTN_BLOCK
    assert nb >= 3
    qw = ATTN_HEADS * HEAD_DIM
    kvw = ATTN_KV_HEADS * HEAD_DIM
    variant = lambda n: jnp.where(n == 0, 0, jnp.where(n == nb - 1, 2, 1))
    return pl.pallas_call(
        functools.partial(_attn_kernel, seq=S),
        grid=(B, nb),
        in_specs=[pl.BlockSpec((None, ATTN_BLOCK, qw), lambda b, n: (b, n, COL_AQ // qw)),
                  pl.BlockSpec((None, S, kvw), lambda b, n: (b, 0, COL_AK // kvw)),
                  pl.BlockSpec((None, S, kvw), lambda b, n: (b, 0, COL_AV // kvw)),
                  pl.BlockSpec((None, ATTN_HEADS, ATTN_BLOCK, 3 * ATTN_BLOCK),
                               lambda b, n: (variant(n), 0, 0, 0)),
                  pl.BlockSpec(sink_b.shape, lambda b, n: (0, 0, 0))],
        out_specs=pl.BlockSpec((None, ATTN_BLOCK, qw), lambda b, n: (b, n, 0)),
        out_shape=jax.ShapeDtypeStruct((B, S, qw), BF16),
        compiler_params=_cparams(("parallel", "arbitrary")),
        name="window_attn",
    )(proj3, proj3, proj3, bias_tab, sink_b)


CONV_COLS = 4 * DN_DIM
CONV_HALO = 16


def _conv_kernel(x_ref, prev_ref, next_ref, w_ref, o_ref, buf_ref, *, ts, n_sblk):
    s = pl.program_id(1)
    c = pl.program_id(2)
    pad = CONV_WIDTH // 2
    prev = jnp.where(s > 0, prev_ref[...].astype(F32), 0.0)
    nxt = jnp.where(s < n_sblk - 1, next_ref[...].astype(F32), 0.0)
    buf_ref[0:CONV_HALO, :] = prev
    buf_ref[CONV_HALO:CONV_HALO + ts, :] = x_ref[...].astype(F32)
    buf_ref[CONV_HALO + ts:CONV_HALO + ts + CONV_HALO, :] = nxt
    y = jnp.zeros((ts, CONV_COLS), F32)
    for w in range(CONV_WIDTH):
        y = y + buf_ref[CONV_HALO + w - pad:CONV_HALO + w - pad + ts, :] * w_ref[w:w + 1, :]
    y = jax.nn.silu(y)
    is_qk = c < 4
    qscale = jnp.where(c < 2, DN_DIM ** -0.5, 1.0).astype(F32)
    for hh in range(CONV_COLS // DN_DIM):
        yh = y[:, hh * DN_DIM:(hh + 1) * DN_DIM]
        nrm = yh * lax.rsqrt(jnp.sum(yh * yh, axis=-1, keepdims=True) + RMS_EPS) * qscale
        o_ref[:, hh * DN_DIM:(hh + 1) * DN_DIM] = jnp.where(is_qk, nrm, yh)


def _conv(proj3, conv_w, ts):
    B, S, _ = proj3.shape
    n_sblk = S // ts
    c0 = COL_DQ // CONV_COLS
    ncb = 3 * DN_HEADS * DN_DIM // CONV_COLS
    hb = ts // CONV_HALO
    nh = S // CONV_HALO
    return pl.pallas_call(
        functools.partial(_conv_kernel, ts=ts, n_sblk=n_sblk),
        grid=(B, n_sblk, ncb),
        in_specs=[pl.BlockSpec((None, ts, CONV_COLS), lambda b, s, c: (b, s, c0 + c)),
                  pl.BlockSpec((None, CONV_HALO, CONV_COLS),
                               lambda b, s, c: (b, jnp.maximum(s * hb - 1, 0), c0 + c)),
                  pl.BlockSpec((None, CONV_HALO, CONV_COLS),
                               lambda b, s, c: (b, jnp.minimum((s + 1) * hb, nh - 1), c0 + c)),
                  pl.BlockSpec((CONV_WIDTH, CONV_COLS), lambda b, s, c: (0, c))],
        out_specs=pl.BlockSpec((None, ts, CONV_COLS), lambda b, s, c: (b, s, c)),
        out_shape=jax.ShapeDtypeStruct((B, S, 3 * DN_HEADS * DN_DIM), F32),
        scratch_shapes=[pltpu.VMEM((ts + 2 * CONV_HALO, CONV_COLS), F32)],
        compiler_params=_cparams(("parallel", "parallel", "arbitrary")),
        name="short_conv",
    )(proj3, proj3, proj3, conv_w)


def _dn_kernel(qf, kf, vf, bgf, bgTf, qb, kb, vb, bgb, bgTb, of_ref, ob_ref, state_ref):
    @pl.when(pl.program_id(1) == 0)
    def _():
        state_ref[...] = jnp.zeros_like(state_ref)

    C = DN_CHUNK
    hi = lax.Precision.HIGHEST
    row = lax.broadcasted_iota(I32, (C, C), 0)
    col = lax.broadcasted_iota(I32, (C, C), 1)
    eye = jnp.where(row == col, 1.0, 0.0)
    refs = ((qf, kf, vf, bgf, bgTf, of_ref), (qb, kb, vb, bgb, bgTb, ob_ref))
    incl, strict, last, sels, bg, g_cum, g_cum_t = [], [], [], [], [], [], []
    for d in range(2):
        if d == 0:
            inc, stc, lst, lo, up = row >= col, row > col, C - 1, col, row
        else:
            inc, stc, lst, lo, up = row <= col, row < col, 0, row, col
        incl.append(inc)
        strict.append(stc)
        last.append(lst)
        sd, half = [], 1
        while half < C:
            sd.append(((row ^ col) < 2 * half) & ((up & half) != 0) & ((lo & half) == 0))
            half *= 2
        sels.append(sd)
        tri = inc.astype(F32)
        bg.append(refs[d][3][...])
        g_cum.append(jnp.dot(tri, bg[d], preferred_element_type=F32, precision=hi))
        g_cum_t.append(_dot_nt(refs[d][4][...], tri, precision=hi))

    P = [(d, h) for d in range(2) for h in range(DN_HEADS)]
    R = range(len(P))
    hs = [slice(h * DN_DIM, (h + 1) * DN_DIM) for _, h in P]
    q_ref = [refs[d][0] for d, _ in P]
    k_ref = [refs[d][1] for d, _ in P]
    v_ref = [refs[d][2] for d, _ in P]
    o_ref = [refs[d][5] for d, _ in P]
    cb = [d * DN_HEADS + h for d, h in P]
    cg = [2 * DN_HEADS + c for c in cb]
    beta_c = [bg[d][:, cb[i]:cb[i] + 1] for i, (d, _) in enumerate(P)]
    g_c = [g_cum[d][:, cg[i]:cg[i] + 1] for i, (d, _) in enumerate(P)]
    g_r = [g_cum_t[d][cg[i]:cg[i] + 1, :] for i, (d, _) in enumerate(P)]
    decay = [jnp.where(incl[P[i][0]], jnp.exp(jnp.where(incl[P[i][0]], g_c[i] - g_r[i], 0.0)), 0.0) for i in R]
    kbeta = [k_ref[i][:, hs[i]] * beta_c[i] for i in R]
    lmat = [jnp.where(strict[P[i][0]],
                      _dot_nt(kbeta[i].astype(BF16), k_ref[i][:, hs[i]].astype(BF16)) * decay[i], 0.0) for i in R]
    x = [eye - jnp.where(sels[P[i][0]][0], lmat[i], 0.0) for i in R]
    for lvl in range(1, len(sels[0])):
        xb = [x[i].astype(BF16) for i in R]
        t = [_dot(xb[i], jnp.where(sels[P[i][0]][lvl], lmat[i], 0.0).astype(BF16)) for i in R]
        x = [x[i] - _dot(t[i].astype(BF16), xb[i]) for i in R]
    e_g = [jnp.exp(g_c[i]) for i in R]
    uw = [_dot(x[i].astype(BF16),
               jnp.concatenate([v_ref[i][:, hs[i]] * beta_c[i], kbeta[i] * e_g[i]], axis=1).astype(BF16)) for i in R]
    amat = [(_dot_nt(q_ref[i][:, hs[i]].astype(BF16), k_ref[i][:, hs[i]].astype(BF16)) * decay[i]).astype(BF16)
            for i in R]
    g_last = [g_c[i][last[P[i][0]]:last[P[i][0]] + 1, :] for i in R]
    state_b = [state_ref[d, h].astype(BF16) for d, h in P]
    v_new = [(uw[i][:, :DN_DIM] - _dot(uw[i][:, DN_DIM:].astype(BF16), state_b[i])).astype(BF16) for i in R]
    for i in R:
        qg = (q_ref[i][:, hs[i]] * e_g[i]).astype(BF16)
        o_ref[i][:, hs[i]] = _dot(qg, state_b[i]) + _dot(amat[i], v_new[i])
    for i, (d, h) in enumerate(P):
        kd = k_ref[i][:, hs[i]] * jnp.exp(g_last[i] - g_c[i])
        state_ref[d, h] = state_ref[d, h] * jnp.exp(g_last[i]) + _dot(kd.T.astype(BF16), v_new[i])


def _deltanet(qkv, bg3, bgT):
    B, S, _ = qkv.shape
    C = DN_CHUNK
    N = S // C
    hw = DN_HEADS * DN_DIM

    def specs(nmap):
        return [pl.BlockSpec((None, C, hw), lambda b, n: (b, nmap(n), 0)),
                pl.BlockSpec((None, C, hw), lambda b, n: (b, nmap(n), 1)),
                pl.BlockSpec((None, C, hw), lambda b, n: (b, nmap(n), 2)),
                pl.BlockSpec((None, C, 128), lambda b, n: (b, nmap(n), 0)),
                pl.BlockSpec((N_TAIL, C), lambda b, n: (0, b * N + nmap(n)))]

    fwd = lambda n: n
    bwd = lambda n: N - 1 - n
    out_sd = jax.ShapeDtypeStruct((B, S, hw), F32)
    return pl.pallas_call(
        _dn_kernel,
        grid=(B, N),
        in_specs=specs(fwd) + specs(bwd),
        out_specs=[pl.BlockSpec((None, C, hw), lambda b, n: (b, fwd(n), 0)),
                   pl.BlockSpec((None, C, hw), lambda b, n: (b, bwd(n), 0))],
        out_shape=[out_sd, out_sd],
        scratch_shapes=[pltpu.VMEM((2, DN_HEADS, DN_DIM, DN_DIM), F32)],
        compiler_params=_cparams(("parallel", "arbitrary")),
        name="deltanet_scan",
    )(qkv, qkv, qkv, bg3, bgT, qkv, qkv, qkv, bg3, bgT)


def _layer_norm(y, g, b):
    mu = jnp.mean(y, axis=-1, keepdims=True)
    var = jnp.mean(jnp.square(y - mu), axis=-1, keepdims=True)
    return (y - mu) * lax.rsqrt(var + LN_EPS) * g + b


def _outproj_kernel(attn_ref, of_ref, ob_ref, z0_ref, z1_ref, x_ref, w_ref, nw_ref, g_ref, b_ref, h_ref, mix_ref):
    aw = ATTN_HEADS * HEAD_DIM
    mix_ref[:, :aw] = attn_ref[...]
    nw = nw_ref[...]
    half = z0_ref.shape[1] // DN_DIM
    for hh in range(DN_HEADS):
        sl = slice(hh * DN_DIM, (hh + 1) * DN_DIM)
        o = of_ref[:, sl] + ob_ref[:, sl]
        z_ref = z0_ref if hh < half else z1_ref
        zc = (hh % half) * DN_DIM
        z = z_ref[:, zc:zc + DN_DIM].astype(F32)
        o = o * lax.rsqrt(jnp.mean(o * o, axis=-1, keepdims=True) + RMS_EPS) * nw
        mix_ref[:, aw + hh * DN_DIM:aw + (hh + 1) * DN_DIM] = (o * jax.nn.silu(z)).astype(BF16)
    y = DEEPNORM_ALPHA * x_ref[...] + _dot(mix_ref[...], w_ref[...])
    h_ref[...] = _layer_norm(y, g_ref[...], b_ref[...])


def _outproj(attn2, of2, ob2, proj2, x2, w_out_b, norm_w, ln_g, ln_b, tm):
    M, D = x2.shape
    dw = DN_HEADS * DN_DIM
    zb = dw // 2
    z0 = COL_DZ // zb
    row = lambda w: pl.BlockSpec((tm, w), lambda i: (i, 0))
    full = lambda shape: pl.BlockSpec(shape, lambda i: (0, 0))
    return pl.pallas_call(
        _outproj_kernel,
        grid=(M // tm,),
        in_specs=[row(ATTN_HEADS * HEAD_DIM), row(dw), row(dw),
                  pl.BlockSpec((tm, zb), lambda i: (i, z0)), pl.BlockSpec((tm, zb), lambda i: (i, z0 + 1)),
                  row(D), full(w_out_b.shape), full((1, DN_DIM)), full((1, D)), full((1, D))],
        out_specs=row(D),
        out_shape=jax.ShapeDtypeStruct((M, D), F32),
        scratch_shapes=[pltpu.VMEM((tm, ATTN_HEADS * HEAD_DIM + dw), BF16)],
        compiler_params=_cparams(("parallel",)),
        name="outproj_ln",
    )(attn2, of2, ob2, proj2, proj2, x2, w_out_b, norm_w.reshape(1, -1), ln_g.reshape(1, -1), ln_b.reshape(1, -1))


TOPK_W = 128


def _topk_rounds(ss, labels, val_refs, idx_refs):
    big = jnp.iinfo(jnp.int32).max
    for r in range(PEER_TOPK):
        ms = [jnp.max(s, axis=0, keepdims=True) for s in ss]
        idxs = [jnp.min(jnp.where(s == m, lab, big), axis=0, keepdims=True) for s, m, lab in zip(ss, ms, labels)]
        for n in range(len(ss)):
            val_refs[n][r:r + 1, :] = ms[n]
            idx_refs[n][r:r + 1, :] = idxs[n]
        ss = [jnp.where(lab == idx, -jnp.inf, s) for s, idx, lab in zip(ss, idxs, labels)]


def _staircase_candidates(sv1, sv2):
    w = sv1.shape[1]
    iota8 = lax.broadcasted_iota(I32, (8, w), 0)
    vals, labs = [], []
    for a in range(8):
        nb = PEER_TOPK // (a + 1)
        for b0 in range(0, nb, 8):
            v = sv1[a:a + 1, :] + sv2[b0:b0 + 8, :]
            if nb - b0 < 8:
                v = jnp.where(iota8 < nb - b0, v, -jnp.inf)
            vals.append(v)
            labs.append(iota8 + (a * PEER_TOPK + b0))
    vals.append(sv1[8:16, :] + sv2[0:1, :])
    labs.append((iota8 + 8) * PEER_TOPK)
    return jnp.concatenate(vals, axis=0), jnp.concatenate(labs, axis=0)


def _pick_rows(sel, table):
    out = jnp.zeros(sel.shape, table.dtype)
    for a in range(PEER_TOPK):
        out = jnp.where(sel == a, table[a:a + 1, :], out)
    return out


def _peer_topk_kernel(h_ref, wq_ref, keys_ref, i_ref, j_ref, g_ref,
                      qs_ref, sv_ref, si_ref, tv_ref, tc_ref, *, tq):
    hd = pl.program_id(1)
    nq = 2 * PEER_HEADS

    @pl.when(hd == 0)
    def _():
        q = _dot(h_ref[...].astype(BF16), wq_ref[...])
        for c in range(nq):
            qs_ref[c] = q[:, c * PEER_HALF:(c + 1) * PEER_HALF].astype(BF16)

    slabs = range(tq // TOPK_W)
    lanes = [slice(sl * TOPK_W, (sl + 1) * TOPK_W) for sl in slabs]
    key_iota = lax.broadcasted_iota(I32, (PEER_NKEYS, TOPK_W), 0)
    probs = [(sl, p) for sl in slabs for p in range(2)]
    scores = [_dot_nt(keys_ref[p], qs_ref[2 * hd + p, lanes[sl], :]) for sl, p in probs]
    _topk_rounds(scores, [key_iota] * len(probs),
                 [sv_ref.at[sl, p] for sl, p in probs], [si_ref.at[sl, p] for sl, p in probs])
    cands = [_staircase_candidates(sv_ref[sl, 0], sv_ref[sl, 1]) for sl in slabs]
    _topk_rounds([c[0] for c in cands], [c[1] for c in cands],
                 [tv_ref.at[sl] for sl in slabs], [tc_ref.at[sl] for sl in slabs])
    shift = int(math.log2(PEER_TOPK))
    for sl in slabs:
        top_s = tv_ref[sl]
        top_c = tc_ref[sl]
        e = jnp.exp(top_s - top_s[0:1, :])
        g_ref[:, lanes[sl]] = e / jnp.sum(e, axis=0, keepdims=True)
        i_ref[:, lanes[sl]] = _pick_rows(top_c >> shift, si_ref[sl, 0])
        j_ref[:, lanes[sl]] = _pick_rows(top_c & (PEER_TOPK - 1), si_ref[sl, 1])


def _peer_topk(h2, wq_b, keys_b, tq):
    M, D = h2.shape
    nq = 2 * PEER_HEADS
    out_spec = pl.BlockSpec((PEER_TOPK, tq), lambda i, hd: (hd, i))
    np_ = PEER_HEADS * PEER_TOPK
    return pl.pallas_call(
        functools.partial(_peer_topk_kernel, tq=tq),
        grid=(M // tq, PEER_HEADS),
        in_specs=[pl.BlockSpec((tq, D), lambda i, hd: (i, 0)),
                  pl.BlockSpec(wq_b.shape, lambda i, hd: (0, 0)),
                  pl.BlockSpec(keys_b.shape, lambda i, hd: (0, 0, 0))],
        out_specs=[out_spec, out_spec, out_spec],
        out_shape=[jax.ShapeDtypeStruct((np_, M), I32), jax.ShapeDtypeStruct((np_, M), I32),
                   jax.ShapeDtypeStruct((np_, M), F32)],
        scratch_shapes=[pltpu.VMEM((nq, tq, PEER_HALF), BF16),
                        pltpu.VMEM((tq // TOPK_W, 2, PEER_TOPK, TOPK_W), F32),
                        pltpu.VMEM((tq // TOPK_W, 2, PEER_TOPK, TOPK_W), I32),
                        pltpu.VMEM((tq // TOPK_W, PEER_TOPK, TOPK_W), F32),
                        pltpu.VMEM((tq // TOPK_W, PEER_TOPK, TOPK_W), I32)],
        compiler_params=_cparams(("parallel", "arbitrary")),
        name="peer_topk",
    )(h2, wq_b, keys_b)


def _gate_kernel(i_ref, j_ref, g_ref, o_ref, *, tg):
    sub = lax.broadcasted_iota(I32, (PEER_NKEYS, PEER_HEADS * PEER_TOPK), 0)
    rows = o_ref.shape[2]

    def body(t, carry):
        irow = i_ref[pl.ds(t, 1), :]
        jrow = j_ref[pl.ds(t, 1), :]
        grow = g_ref[pl.ds(t, 1), :]
        at = jnp.where(sub == irow, grow, 0.0).astype(BF16)
        bt = jnp.where(sub == jrow, 1.0, 0.0).astype(BF16)
        gt = _dot_nt(at, bt)
        for blk in range(o_ref.shape[0]):
            o_ref[blk, t] = gt[blk * rows:(blk + 1) * rows, :]
        return carry

    lax.fori_loop(0, tg, body, 0, unroll=8)


def _gate_matrix(i_n, j_n, g_n, tg, tt, rows):
    M, P = i_n.shape
    nblk = PEER_NKEYS // rows
    per_tile = tt // tg
    spec = pl.BlockSpec((tg, P), lambda i: (i, 0))
    return pl.pallas_call(
        functools.partial(_gate_kernel, tg=tg),
        grid=(M // tg,),
        in_specs=[spec, spec, spec],
        out_specs=pl.BlockSpec((None, nblk, tg, rows, PEER_NKEYS),
                               lambda i: (i // per_tile, 0, i % per_tile, 0, 0)),
        out_shape=jax.ShapeDtypeStruct((M // tt, nblk, tt, rows, PEER_NKEYS), F32),
        compiler_params=_cparams(("parallel",)),
        name="peer_gates",
    )(i_n, j_n, g_n)


def _peer_dense_kernel(h_ref, u_ref, v_ref, gm_ref, g_ref, b_ref, o_ref, hb_ref, p_ref):
    j = pl.program_id(1)

    @pl.when(j == 0)
    def _():
        hb_ref[...] = h_ref[...].astype(BF16)
        o_ref[...] = jnp.zeros_like(o_ref)

    s = _dot_nt(hb_ref[...], u_ref[...])
    act = 0.5 * s * (1.0 + lax.erf(s * (2.0 ** -0.5)))
    tt = s.shape[0]
    rows = gm_ref.shape[0] // tt
    for r in range(rows):
        sl = slice(r * PEER_NKEYS, (r + 1) * PEER_NKEYS)
        p_ref[:, sl] = (act[:, sl] * gm_ref[pl.ds(r, tt, stride=rows), :]).astype(BF16)
    o_ref[...] += _dot(p_ref[...], v_ref[...])

    @pl.when(j == pl.num_programs(1) - 1)
    def _():
        y = DEEPNORM_ALPHA * h_ref[...] + o_ref[...]
        o_ref[...] = _layer_norm(y, g_ref[...], b_ref[...])


def _peer_dense(h2, u_b, v_b, gmat, ln_g, ln_b, tt, eb):
    M, D = h2.shape
    E = u_b.shape[0]
    rows = eb // PEER_NKEYS
    return pl.pallas_call(
        _peer_dense_kernel,
        grid=(M // tt, E // eb),
        in_specs=[pl.BlockSpec((tt, D), lambda i, j: (i, 0)),
                  pl.BlockSpec((eb, D), lambda i, j: (j, 0)),
                  pl.BlockSpec((eb, D), lambda i, j: (j, 0)),
                  pl.BlockSpec((None, None, tt * rows, PEER_NKEYS), lambda i, j: (i, j, 0, 0)),
                  pl.BlockSpec((1, D), lambda i, j: (0, 0)),
                  pl.BlockSpec((1, D), lambda i, j: (0, 0))],
        out_specs=pl.BlockSpec((tt, D), lambda i, j: (i, 0)),
        out_shape=jax.ShapeDtypeStruct((M, D), F32),
        scratch_shapes=[pltpu.VMEM((tt, D), BF16), pltpu.VMEM((tt, eb), BF16)],
        compiler_params=_cparams(("parallel", "arbitrary")),
        name="peer_dense",
    )(h2, u_b, v_b, gmat, ln_g.reshape(1, -1), ln_b.reshape(1, -1))


def _tile(n, pref):
    t = min(n, pref)
    assert n % t == 0
    return t


def _layer(x, w_in, conv_w, a_log, dt_bias, dn_norm_w, attn_sink, rel_bias, w_out,
           ln1_g, ln1_b, peer_wq, peer_keys, peer_u, peer_v, ln2_g, ln2_b):
    B, S, D = x.shape
    T = B * S
    x2 = x.reshape(T, D)

    proj = _inproj(x2, w_in[:, :COL_TAIL].astype(BF16), _tile(T, 1024), 512)
    bg, bgT = _tail(x2, w_in[:, COL_TAIL:], a_log, dt_bias, _tile(T, 1024))
    proj3 = proj.reshape(B, S, -1)

    sink_b = jnp.broadcast_to(attn_sink.astype(F32).reshape(ATTN_KV_HEADS, ATTN_GROUP, 1),
                              (ATTN_KV_HEADS, ATTN_GROUP, 128))
    attn = _attention(proj3, _attn_bias_table(rel_bias), sink_b)

    qkv = _conv(proj3, conv_w, _tile(S, 512))
    o_f, o_b = _deltanet(qkv, bg.reshape(B, S, 128), bgT)

    h = _outproj(attn.reshape(T, -1), o_f.reshape(T, -1), o_b.reshape(T, -1), proj, x2,
                 w_out.astype(BF16), dn_norm_w, ln1_g, ln1_b, _tile(T, 512))

    i_t, j_t, g_t = _peer_topk(h, peer_wq.astype(BF16), peer_keys.astype(BF16), _tile(T, 256))
    tt, eb = _tile(T, 512), 1024
    rows = eb // PEER_NKEYS
    gmat = _gate_matrix(i_t.T, j_t.T, g_t.T, _tile(T, 64), tt, rows)
    gmat = gmat.reshape(T // tt, PEER_NKEYS // rows, tt * rows, PEER_NKEYS)
    y = _peer_dense(h, peer_u.astype(BF16), peer_v.astype(BF16), gmat, ln2_g, ln2_b, tt, eb)
    return y.reshape(B, S, D)


def kernel(x, w_in, conv_w, a_log, dt_bias, dn_norm_w, attn_sink, rel_bias, w_out, ln1_g, ln1_b,
           peer_wq, peer_keys, peer_u, peer_v, ln2_g, ln2_b):
    for l in range(DEPTH):
        x = _layer(x, w_in[l], conv_w[l], a_log[l], dt_bias[l], dn_norm_w[l], attn_sink[l], rel_bias,
                   w_out[l], ln1_g[l], ln1_b[l], peer_wq[l], peer_keys[l], peer_u[l], peer_v[l],
                   ln2_g[l], ln2_b[l])
    return x
```

```python
import functools
import math

import jax
import jax.numpy as jnp
import numpy as np
from jax import lax
from jax.experimental import pallas as pl
from jax.experimental.pallas import tpu as pltpu

F32 = jnp.float32
BF16 = jnp.bfloat16
I32 = jnp.int32

HEAD_DIM = 128
ATTN_HEADS = 8
ATTN_KV_HEADS = 2
ATTN_GROUP = ATTN_HEADS // ATTN_KV_HEADS
WINDOW = 128
ATTN_BLOCK = 128
N_BUCKETS = 32
MAX_DISTANCE = 128
DN_HEADS = 8
DN_DIM = 128
CONV_WIDTH = 5
DN_CHUNK = 128
PEER_HEADS = 8
PEER_NKEYS = 128
PEER_HALF = 128
PEER_TOPK = 16
DEPTH = 1
DEEPNORM_ALPHA = (2.0 * DEPTH) ** 0.25
LN_EPS = 1e-5
RMS_EPS = 1e-6
NEG_BIG = -1e30

COL_AQ = 0
COL_AK = COL_AQ + ATTN_HEADS * HEAD_DIM
COL_AV = COL_AK + ATTN_KV_HEADS * HEAD_DIM
COL_DQ = COL_AV + ATTN_KV_HEADS * HEAD_DIM
COL_DZ = COL_DQ + 3 * DN_HEADS * DN_DIM
COL_TAIL = COL_DZ + DN_HEADS * DN_DIM
N_TAIL = 4 * DN_HEADS

VMEM_LIMIT = 60 * 1024 * 1024


def _cparams(sem):
    return pltpu.CompilerParams(dimension_semantics=sem, vmem_limit_bytes=VMEM_LIMIT)


def _dot(a, b):
    return jnp.dot(a, b, preferred_element_type=F32)


def _dot_nt(a, b, precision=None):
    return lax.dot_general(a, b, (((1,), (1,)), ((), ())), preferred_element_type=F32, precision=precision)


def _inproj_kernel(x_ref, w_ref, o_ref, xb_ref):
    @pl.when(pl.program_id(1) == 0)
    def _():
        xb_ref[...] = x_ref[...].astype(BF16)

    o_ref[...] = _dot(xb_ref[...], w_ref[...]).astype(o_ref.dtype)


def _inproj(x2, w_main, tm, tn):
    M, K = x2.shape
    N = w_main.shape[1]
    return pl.pallas_call(
        _inproj_kernel,
        grid=(M // tm, N // tn),
        in_specs=[pl.BlockSpec((tm, K), lambda i, j: (i, 0)),
                  pl.BlockSpec((K, tn), lambda i, j: (0, j))],
        out_specs=pl.BlockSpec((tm, tn), lambda i, j: (i, j)),
        out_shape=jax.ShapeDtypeStruct((M, N), BF16),
        scratch_shapes=[pltpu.VMEM((tm, K), BF16)],
        compiler_params=_cparams(("parallel", "arbitrary")),
        name="inproj",
    )(x2, w_main)


def _tail_kernel(x_ref, wt_ref, wtT_ref, alog_r_ref, dtb_r_ref, alog_c_ref, dtb_c_ref, bg_ref, bgT_ref):
    xb = x_ref[...].astype(BF16)
    t = _dot(xb, wt_ref[...])
    tT = _dot_nt(wtT_ref[...], xb)
    lane = lax.broadcasted_iota(I32, t.shape, 1)
    g = -jnp.exp(alog_r_ref[...]) * jax.nn.softplus(t + dtb_r_ref[...])
    bg_ref[...] = jnp.where(lane < 2 * DN_HEADS, jax.nn.sigmoid(t), g)
    sub = lax.broadcasted_iota(I32, tT.shape, 0)
    gT = -jnp.exp(alog_c_ref[...]) * jax.nn.softplus(tT + dtb_c_ref[...])
    bgT_ref[...] = jnp.where(sub < 2 * DN_HEADS, jax.nn.sigmoid(tT), gT)


def _tail(x2, w_tail, a_log, dt_bias, tm):
    M, K = x2.shape
    wt = jnp.zeros((K, 128), BF16).at[:, :N_TAIL].set(w_tail.astype(BF16))
    wtT = w_tail.astype(BF16).T
    al = jnp.zeros((N_TAIL,), F32).at[2 * DN_HEADS:].set(a_log.reshape(-1))
    db = jnp.zeros((N_TAIL,), F32).at[2 * DN_HEADS:].set(dt_bias.reshape(-1))
    al_r = jnp.zeros((1, 128), F32).at[0, :N_TAIL].set(al)
    db_r = jnp.zeros((1, 128), F32).at[0, :N_TAIL].set(db)
    full = lambda shape: pl.BlockSpec(shape, lambda i: (0, 0))
    return pl.pallas_call(
        _tail_kernel,
        grid=(M // tm,),
        in_specs=[pl.BlockSpec((tm, K), lambda i: (i, 0)), full((K, 128)), full((N_TAIL, K)),
                  full((1, 128)), full((1, 128)), full((N_TAIL, 1)), full((N_TAIL, 1))],
        out_specs=[pl.BlockSpec((tm, 128), lambda i: (i, 0)), pl.BlockSpec((N_TAIL, tm), lambda i: (0, i))],
        out_shape=[jax.ShapeDtypeStruct((M, 128), F32), jax.ShapeDtypeStruct((N_TAIL, M), F32)],
        compiler_params=_cparams(("parallel",)),
        name="tail",
    )(x2, wt, wtT, al_r, db_r, al.reshape(N_TAIL, 1), db.reshape(N_TAIL, 1))


def _t5_bucket(rel):
    nb = N_BUCKETS // 2
    max_exact = nb // 2
    ret = jnp.where(rel > 0, nb, 0)
    n = jnp.abs(rel)
    large = max_exact + (jnp.log(jnp.maximum(n, 1).astype(F32) / max_exact)
                         / math.log(MAX_DISTANCE / max_exact) * (nb - max_exact)).astype(I32)
    large = jnp.minimum(large, nb - 1)
    return ret + jnp.where(n < max_exact, n, large)


def _attn_bias_table(rel_bias):
    qi = jnp.arange(ATTN_BLOCK)[:, None]
    kj = jnp.arange(3 * ATTN_BLOCK)[None, :]
    rel = jnp.stack([kj - qi - variant * ATTN_BLOCK for variant in range(3)])
    bucket = jnp.where(jnp.abs(rel) <= WINDOW, _t5_bucket(rel), -1)[:, None]
    tab = jnp.full((3, ATTN_HEADS, ATTN_BLOCK, 3 * ATTN_BLOCK), NEG_BIG, F32)
    rb = rel_bias.astype(F32)
    for bk in range(N_BUCKETS):
        tab = jnp.where(bucket == bk, rb[bk][None, :, None, None], tab)
    return tab


def _attn_kernel(q_ref, k_ref, v_ref, bias_ref, sink_ref, o_ref, *, seq):
    n = pl.program_id(1)
    start = jnp.clip((n - 1) * ATTN_BLOCK, 0, seq - 3 * ATTN_BLOCK)
    start = pl.multiple_of(start, ATTN_BLOCK)
    H = range(ATTN_HEADS)
    hs = [slice(h * HEAD_DIM, (h + 1) * HEAD_DIM) for h in H]
    kvs = [slice(g * HEAD_DIM, (g + 1) * HEAD_DIM) for g in range(ATTN_KV_HEADS)]
    kb = [k_ref[pl.ds(start, 3 * ATTN_BLOCK), sl] for sl in kvs]
    vb = [v_ref[pl.ds(start, 3 * ATTN_BLOCK), sl] for sl in kvs]
    s = [_dot_nt(q_ref[:, hs[h]], kb[h // ATTN_GROUP]) * (HEAD_DIM ** -0.5) + bias_ref[h] for h in H]
    sk = [sink_ref[h // ATTN_GROUP, h % ATTN_GROUP:h % ATTN_GROUP + 1, 0:1] for h in H]
    m = [jnp.maximum(jnp.max(s[h], axis=-1, keepdims=True), sk[h]) for h in H]
    p = [jnp.exp(s[h] - m[h]) for h in H]
    den = [jnp.sum(p[h], axis=-1, keepdims=True) + jnp.exp(sk[h] - m[h]) for h in H]
    o = [_dot((p[h] / den[h]).astype(BF16), vb[h // ATTN_GROUP]) for h in H]
    for h in H:
        o_ref[:, hs[h]] = o[h].astype(o_ref.dtype)


def _attention(proj3, bias_tab, sink_b):
    B, S, _ = proj3.shape
    nb = S // ATTN_BLOCK
    assert nb >= 3
    qw = ATTN_HEADS * HEAD_DIM
    kvw = ATTN_KV_HEADS * HEAD_DIM
    variant = lambda n: jnp.where(n == 0, 0, jnp.where(n == nb - 1, 2, 1))
    return pl.pallas_call(
        functools.partial(_attn_kernel, seq=S),
        grid=(B, nb),
        in_specs=[pl.BlockSpec((None, ATTN_BLOCK, qw), lambda b, n: (b, n, COL_AQ // qw)),
                  pl.BlockSpec((None, S, kvw), lambda b, n: (b, 0, COL_AK // kvw)),
                  pl.BlockSpec((None, S, kvw), lambda b, n: (b, 0, COL_AV // kvw)),
                  pl.BlockSpec((None, ATTN_HEADS, ATTN_BLOCK, 3 * ATTN_BLOCK),
                               lambda b, n: (variant(n), 0, 0, 0)),
                  pl.BlockSpec(sink_b.shape, lambda b, n: (0, 0, 0))],
        out_specs=pl.BlockSpec((None, ATTN_BLOCK, qw), lambda b, n: (b, n, 0)),
        out_shape=jax.ShapeDtypeStruct((B, S, qw), BF16),
        compiler_params=_cparams(("parallel", "arbitrary")),
        name="window_attn",
    )(proj3, proj3, proj3, bias_tab, sink_b)


CONV_COLS = 4 * DN_DIM
CONV_HALO = 16


def _conv_kernel(x_ref, prev_ref, next_ref, w_ref, o_ref, buf_ref, *, ts, n_sblk):
    s = pl.program_id(1)
    c = pl.program_id(2)
    pad = CONV_WIDTH // 2
    prev = jnp.where(s > 0, prev_ref[...].astype(F32), 0.0)
    nxt = jnp.where(s < n_sblk - 1, next_ref[...].astype(F32), 0.0)
    buf_ref[0:CONV_HALO, :] = prev
    buf_ref[CONV_HALO:CONV_HALO + ts, :] = x_ref[...].astype(F32)
    buf_ref[CONV_HALO + ts:CONV_HALO + ts + CONV_HALO, :] = nxt
    y = jnp.zeros((ts, CONV_COLS), F32)
    for w in range(CONV_WIDTH):
        y = y + buf_ref[CONV_HALO + w - pad:CONV_HALO + w - pad + ts, :] * w_ref[w:w + 1, :]
    y = jax.nn.silu(y)
    is_qk = c < 4
    qscale = jnp.where(c < 2, DN_DIM ** -0.5, 1.0).astype(F32)
    for hh in range(CONV_COLS // DN_DIM):
        yh = y[:, hh * DN_DIM:(hh + 1) * DN_DIM]
        nrm = yh * lax.rsqrt(jnp.sum(yh * yh, axis=-1, keepdims=True) + RMS_EPS) * qscale
        o_ref[:, hh * DN_DIM:(hh + 1) * DN_DIM] = jnp.where(is_qk, nrm, yh)


def _conv(proj3, conv_w, ts):
    B, S, _ = proj3.shape
    n_sblk = S // ts
    c0 = COL_DQ // CONV_COLS
    ncb = 3 * DN_HEADS * DN_DIM // CONV_COLS
    hb = ts // CONV_HALO
    nh = S // CONV_HALO
    return pl.pallas_call(
        functools.partial(_conv_kernel, ts=ts, n_sblk=n_sblk),
        grid=(B, n_sblk, ncb),
        in_specs=[pl.BlockSpec((None, ts, CONV_COLS), lambda b, s, c: (b, s, c0 + c)),
                  pl.BlockSpec((None, CONV_HALO, CONV_COLS),
                               lambda b, s, c: (b, jnp.maximum(s * hb - 1, 0), c0 + c)),
                  pl.BlockSpec((None, CONV_HALO, CONV_COLS),
                               lambda b, s, c: (b, jnp.minimum((s + 1) * hb, nh - 1), c0 + c)),
                  pl.BlockSpec((CONV_WIDTH, CONV_COLS), lambda b, s, c: (0, c))],
        out_specs=pl.BlockSpec((None, ts, CONV_COLS), lambda b, s, c: (b, s, c)),
        out_shape=jax.ShapeDtypeStruct((B, S, 3 * DN_HEADS * DN_DIM), F32),
        scratch_shapes=[pltpu.VMEM((ts + 2 * CONV_HALO, CONV_COLS), F32)],
        compiler_params=_cparams(("parallel", "parallel", "arbitrary")),
        name="short_conv",
    )(proj3, proj3, proj3, conv_w)


def _dn_kernel(qf, kf, vf, bgf, bgTf, qb, kb, vb, bgb, bgTb, of_ref, ob_ref, state_ref):
    @pl.when(pl.program_id(1) == 0)
    def _():
        state_ref[...] = jnp.zeros_like(state_ref)

    C = DN_CHUNK
    hi = lax.Precision.HIGHEST
    row = lax.broadcasted_iota(I32, (C, C), 0)
    col = lax.broadcasted_iota(I32, (C, C), 1)
    eye = jnp.where(row == col, 1.0, 0.0)
    refs = ((qf, kf, vf, bgf, bgTf, of_ref), (qb, kb, vb, bgb, bgTb, ob_ref))
    incl, strict, last, sels, bg, g_cum, g_cum_t = [], [], [], [], [], [], []
    for d in range(2):
        if d == 0:
            inc, stc, lst, lo, up = row >= col, row > col, C - 1, col, row
        else:
            inc, stc, lst, lo, up = row <= col, row < col, 0, row, col
        incl.append(inc)
        strict.append(stc)
        last.append(lst)
        sd, half = [], 1
        while half < C:
            sd.append(((row ^ col) < 2 * half) & ((up & half) != 0) & ((lo & half) == 0))
            half *= 2
        sels.append(sd)
        tri = inc.astype(F32)
        bg.append(refs[d][3][...])
        g_cum.append(jnp.dot(tri, bg[d], preferred_element_type=F32, precision=hi))
        g_cum_t.append(_dot_nt(refs[d][4][...], tri, precision=hi))

    P = [(d, h) for d in range(2) for h in range(DN_HEADS)]
    R = range(len(P))
    hs = [slice(h * DN_DIM, (h + 1) * DN_DIM) for _, h in P]
    q_ref = [refs[d][0] for d, _ in P]
    k_ref = [refs[d][1] for d, _ in P]
    v_ref = [refs[d][2] for d, _ in P]
    o_ref = [refs[d][5] for d, _ in P]
    cb = [d * DN_HEADS + h for d, h in P]
    cg = [2 * DN_HEADS + c for c in cb]
    beta_c = [bg[d][:, cb[i]:cb[i] + 1] for i, (d, _) in enumerate(P)]
    g_c = [g_cum[d][:, cg[i]:cg[i] + 1] for i, (d, _) in enumerate(P)]
    g_r = [g_cum_t[d][cg[i]:cg[i] + 1, :] for i, (d, _) in enumerate(P)]
    decay = [jnp.where(incl[P[i][0]], jnp.exp(jnp.where(incl[P[i][0]], g_c[i] - g_r[i], 0.0)), 0.0) for i in R]
    kbeta = [k_ref[i][:, hs[i]] * beta_c[i] for i in R]
    lmat = [jnp.where(strict[P[i][0]],
                      _dot_nt(kbeta[i].astype(BF16), k_ref[i][:, hs[i]].astype(BF16)) * decay[i], 0.0) for i in R]
    x = [eye - jnp.where(sels[P[i][0]][0], lmat[i], 0.0) for i in R]
    for lvl in range(1, len(sels[0])):
        xb = [x[i].astype(BF16) for i in R]
        t = [_dot(xb[i], jnp.where(sels[P[i][0]][lvl], lmat[i], 0.0).astype(BF16)) for i in R]
        x = [x[i] - _dot(t[i].astype(BF16), xb[i]) for i in R]
    e_g = [jnp.exp(g_c[i]) for i in R]
    uw = [_dot(x[i].astype(BF16),
               jnp.concatenate([v_ref[i][:, hs[i]] * beta_c[i], kbeta[i] * e_g[i]], axis=1).astype(BF16)) for i in R]
    amat = [(_dot_nt(q_ref[i][:, hs[i]].astype(BF16), k_ref[i][:, hs[i]].astype(BF16)) * decay[i]).astype(BF16)
            for i in R]
    g_last = [g_c[i][last[P[i][0]]:last[P[i][0]] + 1, :] for i in R]
    state_b = [state_ref[d, h].astype(BF16) for d, h in P]
    v_new = [(uw[i][:, :DN_DIM] - _dot(uw[i][:, DN_DIM:].astype(BF16), state_b[i])).astype(BF16) for i in R]
    for i in R:
        qg = (q_ref[i][:, hs[i]] * e_g[i]).astype(BF16)
        o_ref[i][:, hs[i]] = _dot(qg, state_b[i]) + _dot(amat[i], v_new[i])
    for i, (d, h) in enumerate(P):
        kd = k_ref[i][:, hs[i]] * jnp.exp(g_last[i] - g_c[i])
        state_ref[d, h] = state_ref[d, h] * jnp.exp(g_last[i]) + _dot(kd.T.astype(BF16), v_new[i])


def _deltanet(qkv, bg3, bgT):
    B, S, _ = qkv.shape
    C = DN_CHUNK
    N = S // C
    hw = DN_HEADS * DN_DIM

    def specs(nmap):
        return [pl.BlockSpec((None, C, hw), lambda b, n: (b, nmap(n), 0)),
                pl.BlockSpec((None, C, hw), lambda b, n: (b, nmap(n), 1)),
                pl.BlockSpec((None, C, hw), lambda b, n: (b, nmap(n), 2)),
                pl.BlockSpec((None, C, 128), lambda b, n: (b, nmap(n), 0)),
                pl.BlockSpec((N_TAIL, C), lambda b, n: (0, b * N + nmap(n)))]

    fwd = lambda n: n
    bwd = lambda n: N - 1 - n
    out_sd = jax.ShapeDtypeStruct((B, S, hw), F32)
    return pl.pallas_call(
        _dn_kernel,
        grid=(B, N),
        in_specs=specs(fwd) + specs(bwd),
        out_specs=[pl.BlockSpec((None, C, hw), lambda b, n: (b, fwd(n), 0)),
                   pl.BlockSpec((None, C, hw), lambda b, n: (b, bwd(n), 0))],
        out_shape=[out_sd, out_sd],
        scratch_shapes=[pltpu.VMEM((2, DN_HEADS, DN_DIM, DN_DIM), F32)],
        compiler_params=_cparams(("parallel", "arbitrary")),
        name="deltanet_scan",
    )(qkv, qkv, qkv, bg3, bgT, qkv, qkv, qkv, bg3, bgT)


def _layer_norm(y, g, b):
    mu = jnp.mean(y, axis=-1, keepdims=True)
    var = jnp.mean(jnp.square(y - mu), axis=-1, keepdims=True)
    return (y - mu) * lax.rsqrt(var + LN_EPS) * g + b


def _outproj_kernel(attn_ref, of_ref, ob_ref, z0_ref, z1_ref, x_ref, w_ref, nw_ref, g_ref, b_ref, h_ref, mix_ref):
    aw = ATTN_HEADS * HEAD_DIM
    mix_ref[:, :aw] = attn_ref[...]
    nw = nw_ref[...]
    half = z0_ref.shape[1] // DN_DIM
    for hh in range(DN_HEADS):
        sl = slice(hh * DN_DIM, (hh + 1) * DN_DIM)
        o = of_ref[:, sl] + ob_ref[:, sl]
        z_ref = z0_ref if hh < half else z1_ref
        zc = (hh % half) * DN_DIM
        z = z_ref[:, zc:zc + DN_DIM].astype(F32)
        o = o * lax.rsqrt(jnp.mean(o * o, axis=-1, keepdims=True) + RMS_EPS) * nw
        mix_ref[:, aw + hh * DN_DIM:aw + (hh + 1) * DN_DIM] = (o * jax.nn.silu(z)).astype(BF16)
    y = DEEPNORM_ALPHA * x_ref[...] + _dot(mix_ref[...], w_ref[...])
    h_ref[...] = _layer_norm(y, g_ref[...], b_ref[...])


def _outproj(attn2, of2, ob2, proj2, x2, w_out_b, norm_w, ln_g, ln_b, tm):
    M, D = x2.shape
    dw = DN_HEADS * DN_DIM
    zb = dw // 2
    z0 = COL_DZ // zb
    row = lambda w: pl.BlockSpec((tm, w), lambda i: (i, 0))
    full = lambda shape: pl.BlockSpec(shape, lambda i: (0, 0))
    return pl.pallas_call(
        _outproj_kernel,
        grid=(M // tm,),
        in_specs=[row(ATTN_HEADS * HEAD_DIM), row(dw), row(dw),
                  pl.BlockSpec((tm, zb), lambda i: (i, z0)), pl.BlockSpec((tm, zb), lambda i: (i, z0 + 1)),
                  row(D), full(w_out_b.shape), full((1, DN_DIM)), full((1, D)), full((1, D))],
        out_specs=row(D),
        out_shape=jax.ShapeDtypeStruct((M, D), F32),
        scratch_shapes=[pltpu.VMEM((tm, ATTN_HEADS * HEAD_DIM + dw), BF16)],
        compiler_params=_cparams(("parallel",)),
        name="outproj_ln",
    )(attn2, of2, ob2, proj2, proj2, x2, w_out_b, norm_w.reshape(1, -1), ln_g.reshape(1, -1), ln_b.reshape(1, -1))


TOPK_W = 128


def _topk_rounds(ss, labels, val_refs, idx_refs):
    big = jnp.iinfo(jnp.int32).max
    for r in range(PEER_TOPK):
        ms = [jnp.max(s, axis=0, keepdims=True) for s in ss]
        idxs = [jnp.min(jnp.where(s == m, lab, big), axis=0, keepdims=True) for s, m, lab in zip(ss, ms, labels)]
        for n in range(len(ss)):
            val_refs[n][r:r + 1, :] = ms[n]
            idx_refs[n][r:r + 1, :] = idxs[n]
        ss = [jnp.where(lab == idx, -jnp.inf, s) for s, idx, lab in zip(ss, idxs, labels)]


def _staircase_candidates(sv1, sv2):
    w = sv1.shape[1]
    iota8 = lax.broadcasted_iota(I32, (8, w), 0)
    vals, labs = [], []
    for a in range(8):
        nb = PEER_TOPK // (a + 1)
        for b0 in range(0, nb, 8):
            v = sv1[a:a + 1, :] + sv2[b0:b0 + 8, :]
            if nb - b0 < 8:
                v = jnp.where(iota8 < nb - b0, v, -jnp.inf)
            vals.append(v)
            labs.append(iota8 + (a * PEER_TOPK + b0))
    vals.append(sv1[8:16, :] + sv2[0:1, :])
    labs.append((iota8 + 8) * PEER_TOPK)
    return jnp.concatenate(vals, axis=0), jnp.concatenate(labs, axis=0)


def _pick_rows(sel, table):
    out = jnp.zeros(sel.shape, table.dtype)
    for a in range(PEER_TOPK):
        out = jnp.where(sel == a, table[a:a + 1, :], out)
    return out


def _peer_topk_kernel(h_ref, wq_ref, keys_ref, i_ref, j_ref, g_ref,
                      qs_ref, sv_ref, si_ref, tv_ref, tc_ref, *, tq):
    hd = pl.program_id(1)
    nq = 2 * PEER_HEADS

    @pl.when(hd == 0)
    def _():
        q = _dot(h_ref[...].astype(BF16), wq_ref[...])
        for c in range(nq):
            qs_ref[c] = q[:, c * PEER_HALF:(c + 1) * PEER_HALF].astype(BF16)

    slabs = range(tq // TOPK_W)
    lanes = [slice(sl * TOPK_W, (sl + 1) * TOPK_W) for sl in slabs]
    key_iota = lax.broadcasted_iota(I32, (PEER_NKEYS, TOPK_W), 0)
    probs = [(sl, p) for sl in slabs for p in range(2)]
    scores = [_dot_nt(keys_ref[p], qs_ref[2 * hd + p, lanes[sl], :]) for sl, p in probs]
    _topk_rounds(scores, [key_iota] * len(probs),
                 [sv_ref.at[sl, p] for sl, p in probs], [si_ref.at[sl, p] for sl, p in probs])
    cands = [_staircase_candidates(sv_ref[sl, 0], sv_ref[sl, 1]) for sl in slabs]
    _topk_rounds([c[0] for c in cands], [c[1] for c in cands],
                 [tv_ref.at[sl] for sl in slabs], [tc_ref.at[sl] for sl in slabs])
    shift = int(math.log2(PEER_TOPK))
    for sl in slabs:
        top_s = tv_ref[sl]
        top_c = tc_ref[sl]
        e = jnp.exp(top_s - top_s[0:1, :])
        g_ref[:, lanes[sl]] = e / jnp.sum(e, axis=0, keepdims=True)
        i_ref[:, lanes[sl]] = _pick_rows(top_c >> shift, si_ref[sl, 0])
        j_ref[:, lanes[sl]] = _pick_rows(top_c & (PEER_TOPK - 1), si_ref[sl, 1])


def _peer_topk(h2, wq_b, keys_b, tq):
    M, D = h2.shape
    nq = 2 * PEER_HEADS
    out_spec = pl.BlockSpec((PEER_TOPK, tq), lambda i, hd: (hd, i))
    np_ = PEER_HEADS * PEER_TOPK
    return pl.pallas_call(
        functools.partial(_peer_topk_kernel, tq=tq),
        grid=(M // tq, PEER_HEADS),
        in_specs=[pl.BlockSpec((tq, D), lambda i, hd: (i, 0)),
                  pl.BlockSpec(wq_b.shape, lambda i, hd: (0, 0)),
                  pl.BlockSpec(keys_b.shape, lambda i, hd: (0, 0, 0))],
        out_specs=[out_spec, out_spec, out_spec],
        out_shape=[jax.ShapeDtypeStruct((np_, M), I32), jax.ShapeDtypeStruct((np_, M), I32),
                   jax.ShapeDtypeStruct((np_, M), F32)],
        scratch_shapes=[pltpu.VMEM((nq, tq, PEER_HALF), BF16),
                        pltpu.VMEM((tq // TOPK_W, 2, PEER_TOPK, TOPK_W), F32),
                        pltpu.VMEM((tq // TOPK_W, 2, PEER_TOPK, TOPK_W), I32),
                        pltpu.VMEM((tq // TOPK_W, PEER_TOPK, TOPK_W), F32),
                        pltpu.VMEM((tq // TOPK_W, PEER_TOPK, TOPK_W), I32)],
        compiler_params=_cparams(("parallel", "arbitrary")),
        name="peer_topk",
    )(h2, wq_b, keys_b)


def _gate_kernel(i_ref, j_ref, g_ref, o_ref, *, tg):
    sub = lax.broadcasted_iota(I32, (PEER_NKEYS, PEER_HEADS * PEER_TOPK), 0)
    rows = o_ref.shape[2]

    def body(t, carry):
        irow = i_ref[pl.ds(t, 1), :]
        jrow = j_ref[pl.ds(t, 1), :]
        grow = g_ref[pl.ds(t, 1), :]
        at = jnp.where(sub == irow, grow, 0.0).astype(BF16)
        bt = jnp.where(sub == jrow, 1.0, 0.0).astype(BF16)
        gt = _dot_nt(at, bt)
        for blk in range(o_ref.shape[0]):
            o_ref[blk, t] = gt[blk * rows:(blk + 1) * rows, :]
        return carry

    lax.fori_loop(0, tg, body, 0, unroll=True)


def _gate_matrix(i_n, j_n, g_n, tg, tt, rows):
    M, P = i_n.shape
    nblk = PEER_NKEYS // rows
    per_tile = tt // tg
    spec = pl.BlockSpec((tg, P), lambda i: (i, 0))
    return pl.pallas_call(
        functools.partial(_gate_kernel, tg=tg),
        grid=(M // tg,),
        in_specs=[spec, spec, spec],
        out_specs=pl.BlockSpec((None, nblk, tg, rows, PEER_NKEYS),
                               lambda i: (i // per_tile, 0, i % per_tile, 0, 0)),
        out_shape=jax.ShapeDtypeStruct((M // tt, nblk, tt, rows, PEER_NKEYS), F32),
        compiler_params=_cparams(("parallel",)),
        name="peer_gates",
    )(i_n, j_n, g_n)


def _peer_dense_kernel(h_ref, u_ref, v_ref, gm_ref, g_ref, b_ref, o_ref, hb_ref, p_ref):
    j = pl.program_id(1)

    @pl.when(j == 0)
    def _():
        hb_ref[...] = h_ref[...].astype(BF16)
        o_ref[...] = jnp.zeros_like(o_ref)

    s = _dot_nt(hb_ref[...], u_ref[...])
    act = 0.5 * s * (1.0 + lax.erf(s * (2.0 ** -0.5)))
    tt = s.shape[0]
    rows = gm_ref.shape[0] // tt
    for r in range(rows):
        sl = slice(r * PEER_NKEYS, (r + 1) * PEER_NKEYS)
        p_ref[:, sl] = (act[:, sl] * gm_ref[pl.ds(r, tt, stride=rows), :]).astype(BF16)
    o_ref[...] += _dot(p_ref[...], v_ref[...])

    @pl.when(j == pl.num_programs(1) - 1)
    def _():
        y = DEEPNORM_ALPHA * h_ref[...] + o_ref[...]
        o_ref[...] = _layer_norm(y, g_ref[...], b_ref[...])


def _peer_dense(h2, u_b, v_b, gmat, ln_g, ln_b, tt, eb):
    M, D = h2.shape
    E = u_b.shape[0]
    rows = eb // PEER_NKEYS
    return pl.pallas_call(
        _peer_dense_kernel,
        grid=(M // tt, E // eb),
        in_specs=[pl.BlockSpec((tt, D), lambda i, j: (i, 0)),
                  pl.BlockSpec((eb, D), lambda i, j: (j, 0)),
                  pl.BlockSpec((eb, D), lambda i, j: (j, 0)),
                  pl.BlockSpec((None, None, tt * rows, PEER_NKEYS), lambda i, j: (i, j, 0, 0)),
                  pl.BlockSpec((1, D), lambda i, j: (0, 0)),
                  pl.BlockSpec((1, D), lambda i, j: (0, 0))],
        out_specs=pl.BlockSpec((tt, D), lambda i, j: (i, 0)),
        out_shape=jax.ShapeDtypeStruct((M, D), F32),
        scratch_shapes=[pltpu.VMEM((tt, D), BF16), pltpu.VMEM((tt, eb), BF16)],
        compiler_params=_cparams(("parallel", "arbitrary")),
        name="peer_dense",
    )(h2, u_b, v_b, gmat, ln_g.reshape(1, -1), ln_b.reshape(1, -1))


def _tile(n, pref):
    t = min(n, pref)
    assert n % t == 0
    return t


def _layer(x, w_in, conv_w, a_log, dt_bias, dn_norm_w, attn_sink, rel_bias, w_out,
           ln1_g, ln1_b, peer_wq, peer_keys, peer_u, peer_v, ln2_g, ln2_b):
    B, S, D = x.shape
    T = B * S
    x2 = x.reshape(T, D)

    proj = _inproj(x2, w_in[:, :COL_TAIL].astype(BF16), _tile(T, 1024), 512)
    bg, bgT = _tail(x2, w_in[:, COL_TAIL:], a_log, dt_bias, _tile(T, 1024))
    proj3 = proj.reshape(B, S, -1)

    sink_b = jnp.broadcast_to(attn_sink.astype(F32).reshape(ATTN_KV_HEADS, ATTN_GROUP, 1),
                              (ATTN_KV_HEADS, ATTN_GROUP, 128))
    attn = _attention(proj3, _attn_bias_table(rel_bias), sink_b)

    qkv = _conv(proj3, conv_w, _tile(S, 512))
    o_f, o_b = _deltanet(qkv, bg.reshape(B, S, 128), bgT)

    h = _outproj(attn.reshape(T, -1), o_f.reshape(T, -1), o_b.reshape(T, -1), proj, x2,
                 w_out.astype(BF16), dn_norm_w, ln1_g, ln1_b, _tile(T, 512))

    i_t, j_t, g_t = _peer_topk(h, peer_wq.astype(BF16), peer_keys.astype(BF16), _tile(T, 256))
    tt, eb = _tile(T, 1024), 512
    rows = eb // PEER_NKEYS
    gmat = _gate_matrix(i_t.T, j_t.T, g_t.T, _tile(T, 64), tt, rows)
    gmat = gmat.reshape(T // tt, PEER_NKEYS // rows, tt * rows, PEER_NKEYS)
    y = _peer_dense(h, peer_u.astype(BF16), peer_v.astype(BF16), gmat, ln2_g, ln2_b, tt, eb)
    return y.reshape(B, S, D)


def kernel(x, w_in, conv_w, a_log, dt_bias, dn_norm_w, attn_sink, rel_bias, w_out, ln1_g, ln1_b,
           peer_wq, peer_keys, peer_u, peer_v, ln2_g, ln2_b):
    for l in range(DEPTH):
        x = _layer(x, w_in[l], conv_w[l], a_log[l], dt_bias[l], dn_norm_w[l], attn_sink[l], rel_bias,
                   w_out[l], ln1_g[l], ln1_b[l], peer_wq[l], peer_keys[l], peer_u[l], peer_v[l],
                   ln2_g[l], ln2_b[l])
    return x
```

```python
import functools
import math

import jax
import jax.numpy as jnp
import numpy as np
from jax import lax
from jax.experimental import pallas as pl
from jax.experimental.pallas import tpu as pltpu

F32 = jnp.float32
BF16 = jnp.bfloat16
I32 = jnp.int32

HEAD_DIM = 128
ATTN_HEADS = 8
ATTN_KV_HEADS = 2
ATTN_GROUP = ATTN_HEADS // ATTN_KV_HEADS
WINDOW = 128
ATTN_BLOCK = 128
N_BUCKETS = 32
MAX_DISTANCE = 128
DN_HEADS = 8
DN_DIM = 128
CONV_WIDTH = 5
DN_CHUNK = 128
PEER_HEADS = 8
PEER_NKEYS = 128
PEER_HALF = 128
PEER_TOPK = 16
DEPTH = 1
DEEPNORM_ALPHA = (2.0 * DEPTH) ** 0.25
LN_EPS = 1e-5
RMS_EPS = 1e-6
NEG_BIG = -1e30

COL_AQ = 0
COL_AK = COL_AQ + ATTN_HEADS * HEAD_DIM
COL_AV = COL_AK + ATTN_KV_HEADS * HEAD_DIM
COL_DQ = COL_AV + ATTN_KV_HEADS * HEAD_DIM
COL_DZ = COL_DQ + 3 * DN_HEADS * DN_DIM
COL_TAIL = COL_DZ + DN_HEADS * DN_DIM
N_TAIL = 4 * DN_HEADS

VMEM_LIMIT = 60 * 1024 * 1024


def _cparams(sem):
    return pltpu.CompilerParams(dimension_semantics=sem, vmem_limit_bytes=VMEM_LIMIT)


def _dot(a, b):
    return jnp.dot(a, b, preferred_element_type=F32)


def _dot_nt(a, b, precision=None):
    return lax.dot_general(a, b, (((1,), (1,)), ((), ())), preferred_element_type=F32, precision=precision)


def _inproj_kernel(x_ref, w_ref, o_ref, xb_ref):
    @pl.when(pl.program_id(1) == 0)
    def _():
        xb_ref[...] = x_ref[...].astype(BF16)

    o_ref[...] = _dot(xb_ref[...], w_ref[...]).astype(o_ref.dtype)


def _inproj(x2, w_main, tm, tn):
    M, K = x2.shape
    N = w_main.shape[1]
    return pl.pallas_call(
        _inproj_kernel,
        grid=(M // tm, N // tn),
        in_specs=[pl.BlockSpec((tm, K), lambda i, j: (i, 0)),
                  pl.BlockSpec((K, tn), lambda i, j: (0, j))],
        out_specs=pl.BlockSpec((tm, tn), lambda i, j: (i, j)),
        out_shape=jax.ShapeDtypeStruct((M, N), BF16),
        scratch_shapes=[pltpu.VMEM((tm, K), BF16)],
        compiler_params=_cparams(("parallel", "arbitrary")),
        name="inproj",
    )(x2, w_main)


def _tail_kernel(x_ref, wt_ref, wtT_ref, alog_r_ref, dtb_r_ref, alog_c_ref, dtb_c_ref, bg_ref, bgT_ref):
    xb = x_ref[...].astype(BF16)
    t = _dot(xb, wt_ref[...])
    tT = _dot_nt(wtT_ref[...], xb)
    lane = lax.broadcasted_iota(I32, t.shape, 1)
    g = -jnp.exp(alog_r_ref[...]) * jax.nn.softplus(t + dtb_r_ref[...])
    bg_ref[...] = jnp.where(lane < 2 * DN_HEADS, jax.nn.sigmoid(t), g)
    sub = lax.broadcasted_iota(I32, tT.shape, 0)
    gT = -jnp.exp(alog_c_ref[...]) * jax.nn.softplus(tT + dtb_c_ref[...])
    bgT_ref[...] = jnp.where(sub < 2 * DN_HEADS, jax.nn.sigmoid(tT), gT)


def _tail(x2, w_tail, a_log, dt_bias, tm):
    M, K = x2.shape
    wt = jnp.zeros((K, 128), BF16).at[:, :N_TAIL].set(w_tail.astype(BF16))
    wtT = w_tail.astype(BF16).T
    al = jnp.zeros((N_TAIL,), F32).at[2 * DN_HEADS:].set(a_log.reshape(-1))
    db = jnp.zeros((N_TAIL,), F32).at[2 * DN_HEADS:].set(dt_bias.reshape(-1))
    al_r = jnp.zeros((1, 128), F32).at[0, :N_TAIL].set(al)
    db_r = jnp.zeros((1, 128), F32).at[0, :N_TAIL].set(db)
    full = lambda shape: pl.BlockSpec(shape, lambda i: (0, 0))
    return pl.pallas_call(
        _tail_kernel,
        grid=(M // tm,),
        in_specs=[pl.BlockSpec((tm, K), lambda i: (i, 0)), full((K, 128)), full((N_TAIL, K)),
                  full((1, 128)), full((1, 128)), full((N_TAIL, 1)), full((N_TAIL, 1))],
        out_specs=[pl.BlockSpec((tm, 128), lambda i: (i, 0)), pl.BlockSpec((N_TAIL, tm), lambda i: (0, i))],
        out_shape=[jax.ShapeDtypeStruct((M, 128), F32), jax.ShapeDtypeStruct((N_TAIL, M), F32)],
        compiler_params=_cparams(("parallel",)),
        name="tail",
    )(x2, wt, wtT, al_r, db_r, al.reshape(N_TAIL, 1), db.reshape(N_TAIL, 1))


def _t5_bucket(rel):
    nb = N_BUCKETS // 2
    max_exact = nb // 2
    ret = jnp.where(rel > 0, nb, 0)
    n = jnp.abs(rel)
    large = max_exact + (jnp.log(jnp.maximum(n, 1).astype(F32) / max_exact)
                         / math.log(MAX_DISTANCE / max_exact) * (nb - max_exact)).astype(I32)
    large = jnp.minimum(large, nb - 1)
    return ret + jnp.where(n < max_exact, n, large)


def _attn_bias_table(rel_bias):
    qi = jnp.arange(ATTN_BLOCK)[:, None]
    kj = jnp.arange(3 * ATTN_BLOCK)[None, :]
    rel = jnp.stack([kj - qi - variant * ATTN_BLOCK for variant in range(3)])
    bucket = jnp.where(jnp.abs(rel) <= WINDOW, _t5_bucket(rel), -1)[:, None]
    tab = jnp.full((3, ATTN_HEADS, ATTN_BLOCK, 3 * ATTN_BLOCK), NEG_BIG, F32)
    rb = rel_bias.astype(F32)
    for bk in range(N_BUCKETS):
        tab = jnp.where(bucket == bk, rb[bk][None, :, None, None], tab)
    return tab


def _attn_kernel(q_ref, k_ref, v_ref, bias_ref, sink_ref, o_ref, *, seq):
    n = pl.program_id(1)
    start = jnp.clip((n - 1) * ATTN_BLOCK, 0, seq - 3 * ATTN_BLOCK)
    start = pl.multiple_of(start, ATTN_BLOCK)
    H = range(ATTN_HEADS)
    hs = [slice(h * HEAD_DIM, (h + 1) * HEAD_DIM) for h in H]
    kvs = [slice(g * HEAD_DIM, (g + 1) * HEAD_DIM) for g in range(ATTN_KV_HEADS)]
    kb = [k_ref[pl.ds(start, 3 * ATTN_BLOCK), sl] for sl in kvs]
    vb = [v_ref[pl.ds(start, 3 * ATTN_BLOCK), sl] for sl in kvs]
    s = [_dot_nt(q_ref[:, hs[h]], kb[h // ATTN_GROUP]) * (HEAD_DIM ** -0.5) + bias_ref[h] for h in H]
    sk = [sink_ref[h // ATTN_GROUP, h % ATTN_GROUP:h % ATTN_GROUP + 1, 0:1] for h in H]
    m = [jnp.maximum(jnp.max(s[h], axis=-1, keepdims=True), sk[h]) for h in H]
    p = [jnp.exp(s[h] - m[h]) for h in H]
    den = [jnp.sum(p[h], axis=-1, keepdims=True) + jnp.exp(sk[h] - m[h]) for h in H]
    o = [_dot((p[h] / den[h]).astype(BF16), vb[h // ATTN_GROUP]) for h in H]
    for h in H:
        o_ref[:, hs[h]] = o[h].astype(o_ref.dtype)


def _attention(proj3, bias_tab, sink_b):
    B, S, _ = proj3.shape
    nb = S // ATTN_BLOCK
    assert nb >= 3
    qw = ATTN_HEADS * HEAD_DIM
    kvw = ATTN_KV_HEADS * HEAD_DIM
    variant = lambda n: jnp.where(n == 0, 0, jnp.where(n == nb - 1, 2, 1))
    return pl.pallas_call(
        functools.partial(_attn_kernel, seq=S),
        grid=(B, nb),
        in_specs=[pl.BlockSpec((None, ATTN_BLOCK, qw), lambda b, n: (b, n, COL_AQ // qw)),
                  pl.BlockSpec((None, S, kvw), lambda b, n: (b, 0, COL_AK // kvw)),
                  pl.BlockSpec((None, S, kvw), lambda b, n: (b, 0, COL_AV // kvw)),
                  pl.BlockSpec((None, ATTN_HEADS, ATTN_BLOCK, 3 * ATTN_BLOCK),
                               lambda b, n: (variant(n), 0, 0, 0)),
                  pl.BlockSpec(sink_b.shape, lambda b, n: (0, 0, 0))],
        out_specs=pl.BlockSpec((None, ATTN_BLOCK, qw), lambda b, n: (b, n, 0)),
        out_shape=jax.ShapeDtypeStruct((B, S, qw), BF16),
        compiler_params=_cparams(("parallel", "arbitrary")),
        name="window_attn",
    )(proj3, proj3, proj3, bias_tab, sink_b)


CONV_COLS = 4 * DN_DIM
CONV_HALO = 16


def _conv_kernel(x_ref, prev_ref, next_ref, w_ref, o_ref, buf_ref, *, ts, n_sblk):
    s = pl.program_id(1)
    c = pl.program_id(2)
    pad = CONV_WIDTH // 2
    prev = jnp.where(s > 0, prev_ref[...].astype(F32), 0.0)
    nxt = jnp.where(s < n_sblk - 1, next_ref[...].astype(F32), 0.0)
    buf_ref[0:CONV_HALO, :] = prev
    buf_ref[CONV_HALO:CONV_HALO + ts, :] = x_ref[...].astype(F32)
    buf_ref[CONV_HALO + ts:CONV_HALO + ts + CONV_HALO, :] = nxt
    y = jnp.zeros((ts, CONV_COLS), F32)
    for w in range(CONV_WIDTH):
        y = y + buf_ref[CONV_HALO + w - pad:CONV_HALO + w - pad + ts, :] * w_ref[w:w + 1, :]
    y = jax.nn.silu(y)
    is_qk = c < 4
    qscale = jnp.where(c < 2, DN_DIM ** -0.5, 1.0).astype(F32)
    for hh in range(CONV_COLS // DN_DIM):
        yh = y[:, hh * DN_DIM:(hh + 1) * DN_DIM]
        nrm = yh * lax.rsqrt(jnp.sum(yh * yh, axis=-1, keepdims=True) + RMS_EPS) * qscale
        o_ref[:, hh * DN_DIM:(hh + 1) * DN_DIM] = jnp.where(is_qk, nrm, yh)


def _conv(proj3, conv_w, ts):
    B, S, _ = proj3.shape
    n_sblk = S // ts
    c0 = COL_DQ // CONV_COLS
    ncb = 3 * DN_HEADS * DN_DIM // CONV_COLS
    hb = ts // CONV_HALO
    nh = S // CONV_HALO
    return pl.pallas_call(
        functools.partial(_conv_kernel, ts=ts, n_sblk=n_sblk),
        grid=(B, n_sblk, ncb),
        in_specs=[pl.BlockSpec((None, ts, CONV_COLS), lambda b, s, c: (b, s, c0 + c)),
                  pl.BlockSpec((None, CONV_HALO, CONV_COLS),
                               lambda b, s, c: (b, jnp.maximum(s * hb - 1, 0), c0 + c)),
                  pl.BlockSpec((None, CONV_HALO, CONV_COLS),
                               lambda b, s, c: (b, jnp.minimum((s + 1) * hb, nh - 1), c0 + c)),
                  pl.BlockSpec((CONV_WIDTH, CONV_COLS), lambda b, s, c: (0, c))],
        out_specs=pl.BlockSpec((None, ts, CONV_COLS), lambda b, s, c: (b, s, c)),
        out_shape=jax.ShapeDtypeStruct((B, S, 3 * DN_HEADS * DN_DIM), F32),
        scratch_shapes=[pltpu.VMEM((ts + 2 * CONV_HALO, CONV_COLS), F32)],
        compiler_params=_cparams(("parallel", "parallel", "arbitrary")),
        name="short_conv",
    )(proj3, proj3, proj3, conv_w)


def _dn_kernel(qf, kf, vf, bgf, bgTf, qb, kb, vb, bgb, bgTb, of_ref, ob_ref, state_ref):
    @pl.when(pl.program_id(1) == 0)
    def _():
        state_ref[...] = jnp.zeros_like(state_ref)

    C = DN_CHUNK
    hi = lax.Precision.HIGHEST
    row = lax.broadcasted_iota(I32, (C, C), 0)
    col = lax.broadcasted_iota(I32, (C, C), 1)
    eye = jnp.where(row == col, 1.0, 0.0)
    refs = ((qf, kf, vf, bgf, bgTf, of_ref), (qb, kb, vb, bgb, bgTb, ob_ref))
    incl, strict, last, sels, bg, g_cum, g_cum_t = [], [], [], [], [], [], []
    for d in range(2):
        if d == 0:
            inc, stc, lst, lo, up = row >= col, row > col, C - 1, col, row
        else:
            inc, stc, lst, lo, up = row <= col, row < col, 0, row, col
        incl.append(inc)
        strict.append(stc)
        last.append(lst)
        sd, half = [], 1
        while half < C:
            sd.append(((row ^ col) < 2 * half) & ((up & half) != 0) & ((lo & half) == 0))
            half *= 2
        sels.append(sd)
        tri = inc.astype(F32)
        bg.append(refs[d][3][...])
        g_cum.append(jnp.dot(tri, bg[d], preferred_element_type=F32, precision=hi))
        g_cum_t.append(_dot_nt(refs[d][4][...], tri, precision=hi))

    P = [(d, h) for d in range(2) for h in range(DN_HEADS)]
    R = range(len(P))
    hs = [slice(h * DN_DIM, (h + 1) * DN_DIM) for _, h in P]
    q_ref = [refs[d][0] for d, _ in P]
    k_ref = [refs[d][1] for d, _ in P]
    v_ref = [refs[d][2] for d, _ in P]
    o_ref = [refs[d][5] for d, _ in P]
    cb = [d * DN_HEADS + h for d, h in P]
    cg = [2 * DN_HEADS + c for c in cb]
    beta_c = [bg[d][:, cb[i]:cb[i] + 1] for i, (d, _) in enumerate(P)]
    g_c = [g_cum[d][:, cg[i]:cg[i] + 1] for i, (d, _) in enumerate(P)]
    g_r = [g_cum_t[d][cg[i]:cg[i] + 1, :] for i, (d, _) in enumerate(P)]
    decay = [jnp.where(incl[P[i][0]], jnp.exp(jnp.where(incl[P[i][0]], g_c[i] - g_r[i], 0.0)), 0.0) for i in R]
    kbeta = [k_ref[i][:, hs[i]] * beta_c[i] for i in R]
    lmat = [jnp.where(strict[P[i][0]],
                      _dot_nt(kbeta[i].astype(BF16), k_ref[i][:, hs[i]].astype(BF16)) * decay[i], 0.0) for i in R]
    x = [eye - jnp.where(sels[P[i][0]][0], lmat[i], 0.0) for i in R]
    for lvl in range(1, len(sels[0])):
        xb = [x[i].astype(BF16) for i in R]
        t = [_dot(xb[i], jnp.where(sels[P[i][0]][lvl], lmat[i], 0.0).astype(BF16)) for i in R]
        x = [x[i] - _dot(t[i].astype(BF16), xb[i]) for i in R]
    e_g = [jnp.exp(g_c[i]) for i in R]
    uw = [_dot(x[i].astype(BF16),
               jnp.concatenate([v_ref[i][:, hs[i]] * beta_c[i], kbeta[i] * e_g[i]], axis=1).astype(BF16)) for i in R]
    amat = [(_dot_nt(q_ref[i][:, hs[i]].astype(BF16), k_ref[i][:, hs[i]].astype(BF16)) * decay[i]).astype(BF16)
            for i in R]
    g_last = [g_c[i][last[P[i][0]]:last[P[i][0]] + 1, :] for i in R]
    state_b = [state_ref[d, h].astype(BF16) for d, h in P]
    v_new = [(uw[i][:, :DN_DIM] - _dot(uw[i][:, DN_DIM:].astype(BF16), state_b[i])).astype(BF16) for i in R]
    for i in R:
        qg = (q_ref[i][:, hs[i]] * e_g[i]).astype(BF16)
        o_ref[i][:, hs[i]] = _dot(qg, state_b[i]) + _dot(amat[i], v_new[i])
    for i, (d, h) in enumerate(P):
        kd = k_ref[i][:, hs[i]] * jnp.exp(g_last[i] - g_c[i])
        state_ref[d, h] = state_ref[d, h] * jnp.exp(g_last[i]) + _dot(kd.T.astype(BF16), v_new[i])


def _deltanet(qkv, bg3, bgT):
    B, S, _ = qkv.shape
    C = DN_CHUNK
    N = S // C
    hw = DN_HEADS * DN_DIM

    def specs(nmap):
        return [pl.BlockSpec((None, C, hw), lambda b, n: (b, nmap(n), 0)),
                pl.BlockSpec((None, C, hw), lambda b, n: (b, nmap(n), 1)),
                pl.BlockSpec((None, C, hw), lambda b, n: (b, nmap(n), 2)),
                pl.BlockSpec((None, C, 128), lambda b, n: (b, nmap(n), 0)),
                pl.BlockSpec((N_TAIL, C), lambda b, n: (0, b * N + nmap(n)))]

    fwd = lambda n: n
    bwd = lambda n: N - 1 - n
    out_sd = jax.ShapeDtypeStruct((B, S, hw), F32)
    return pl.pallas_call(
        _dn_kernel,
        grid=(B, N),
        in_specs=specs(fwd) + specs(bwd),
        out_specs=[pl.BlockSpec((None, C, hw), lambda b, n: (b, fwd(n), 0)),
                   pl.BlockSpec((None, C, hw), lambda b, n: (b, bwd(n), 0))],
        out_shape=[out_sd, out_sd],
        scratch_shapes=[pltpu.VMEM((2, DN_HEADS, DN_DIM, DN_DIM), F32)],
        compiler_params=_cparams(("parallel", "arbitrary")),
        name="deltanet_scan",
    )(qkv, qkv, qkv, bg3, bgT, qkv, qkv, qkv, bg3, bgT)


def _layer_norm(y, g, b):
    mu = jnp.mean(y, axis=-1, keepdims=True)
    var = jnp.mean(jnp.square(y - mu), axis=-1, keepdims=True)
    return (y - mu) * lax.rsqrt(var + LN_EPS) * g + b


def _outproj_kernel(attn_ref, of_ref, ob_ref, z0_ref, z1_ref, x_ref, w_ref, nw_ref, g_ref, b_ref, h_ref, mix_ref):
    aw = ATTN_HEADS * HEAD_DIM
    mix_ref[:, :aw] = attn_ref[...]
    nw = nw_ref[...]
    half = z0_ref.shape[1] // DN_DIM
    for hh in range(DN_HEADS):
        sl = slice(hh * DN_DIM, (hh + 1) * DN_DIM)
        o = of_ref[:, sl] + ob_ref[:, sl]
        z_ref = z0_ref if hh < half else z1_ref
        zc = (hh % half) * DN_DIM
        z = z_ref[:, zc:zc + DN_DIM].astype(F32)
        o = o * lax.rsqrt(jnp.mean(o * o, axis=-1, keepdims=True) + RMS_EPS) * nw
        mix_ref[:, aw + hh * DN_DIM:aw + (hh + 1) * DN_DIM] = (o * jax.nn.silu(z)).astype(BF16)
    y = DEEPNORM_ALPHA * x_ref[...] + _dot(mix_ref[...], w_ref[...])
    h_ref[...] = _layer_norm(y, g_ref[...], b_ref[...])


def _outproj(attn2, of2, ob2, proj2, x2, w_out_b, norm_w, ln_g, ln_b, tm):
    M, D = x2.shape
    dw = DN_HEADS * DN_DIM
    zb = dw // 2
    z0 = COL_DZ // zb
    row = lambda w: pl.BlockSpec((tm, w), lambda i: (i, 0))
    full = lambda shape: pl.BlockSpec(shape, lambda i: (0, 0))
    return pl.pallas_call(
        _outproj_kernel,
        grid=(M // tm,),
        in_specs=[row(ATTN_HEADS * HEAD_DIM), row(dw), row(dw),
                  pl.BlockSpec((tm, zb), lambda i: (i, z0)), pl.BlockSpec((tm, zb), lambda i: (i, z0 + 1)),
                  row(D), full(w_out_b.shape), full((1, DN_DIM)), full((1, D)), full((1, D))],
        out_specs=row(D),
        out_shape=jax.ShapeDtypeStruct((M, D), F32),
        scratch_shapes=[pltpu.VMEM((tm, ATTN_HEADS * HEAD_DIM + dw), BF16)],
        compiler_params=_cparams(("parallel",)),
        name="outproj_ln",
    )(attn2, of2, ob2, proj2, proj2, x2, w_out_b, norm_w.reshape(1, -1), ln_g.reshape(1, -1), ln_b.reshape(1, -1))


TOPK_W = 128
TOPK_HEADS = 2


def _topk_rounds(ss, labels, val_refs, idx_refs):
    big = jnp.iinfo(jnp.int32).max
    for r in range(PEER_TOPK):
        ms = [jnp.max(s, axis=0, keepdims=True) for s in ss]
        idxs = [jnp.min(jnp.where(s == m, lab, big), axis=0, keepdims=True) for s, m, lab in zip(ss, ms, labels)]
        for n in range(len(ss)):
            val_refs[n][r:r + 1, :] = ms[n]
            idx_refs[n][r:r + 1, :] = idxs[n]
        ss = [jnp.where(lab == idx, -jnp.inf, s) for s, idx, lab in zip(ss, idxs, labels)]


def _staircase_candidates(sv1, sv2):
    w = sv1.shape[1]
    iota8 = lax.broadcasted_iota(I32, (8, w), 0)
    vals, labs = [], []
    for a in range(8):
        nb = PEER_TOPK // (a + 1)
        for b0 in range(0, nb, 8):
            v = sv1[a:a + 1, :] + sv2[b0:b0 + 8, :]
            if nb - b0 < 8:
                v = jnp.where(iota8 < nb - b0, v, -jnp.inf)
            vals.append(v)
            labs.append(iota8 + (a * PEER_TOPK + b0))
    vals.append(sv1[8:16, :] + sv2[0:1, :])
    labs.append((iota8 + 8) * PEER_TOPK)
    return jnp.concatenate(vals, axis=0), jnp.concatenate(labs, axis=0)


def _pick_rows(sel, table):
    out = jnp.zeros(sel.shape, table.dtype)
    for a in range(PEER_TOPK):
        out = jnp.where(sel == a, table[a:a + 1, :], out)
    return out


def _peer_topk_kernel(h_ref, wq_ref, keys_ref, i_ref, j_ref, g_ref,
                      qs_ref, sv_ref, si_ref, tv_ref, tc_ref, *, tq):
    hd = pl.program_id(1)
    nq = 2 * PEER_HEADS

    @pl.when(hd == 0)
    def _():
        q = _dot(h_ref[...].astype(BF16), wq_ref[...])
        for c in range(nq):
            qs_ref[c] = q[:, c * PEER_HALF:(c + 1) * PEER_HALF].astype(BF16)

    units = [(hh, sl) for hh in range(TOPK_HEADS) for sl in range(tq // TOPK_W)]
    U = range(len(units))
    lanes = [slice(sl * TOPK_W, (sl + 1) * TOPK_W) for _, sl in units]
    rows = [slice(hh * PEER_TOPK, (hh + 1) * PEER_TOPK) for hh, _ in units]
    key_iota = lax.broadcasted_iota(I32, (PEER_NKEYS, TOPK_W), 0)
    probs = [(u, p) for u in U for p in range(2)]
    scores = [_dot_nt(keys_ref[p], qs_ref[2 * (TOPK_HEADS * hd + units[u][0]) + p, lanes[u], :])
              for u, p in probs]
    _topk_rounds(scores, [key_iota] * len(probs),
                 [sv_ref.at[u, p] for u, p in probs], [si_ref.at[u, p] for u, p in probs])
    cands = [_staircase_candidates(sv_ref[u, 0], sv_ref[u, 1]) for u in U]
    _topk_rounds([c[0] for c in cands], [c[1] for c in cands],
                 [tv_ref.at[u] for u in U], [tc_ref.at[u] for u in U])
    shift = int(math.log2(PEER_TOPK))
    for u in U:
        top_s = tv_ref[u]
        top_c = tc_ref[u]
        e = jnp.exp(top_s - top_s[0:1, :])
        g_ref[rows[u], lanes[u]] = e / jnp.sum(e, axis=0, keepdims=True)
        i_ref[rows[u], lanes[u]] = _pick_rows(top_c >> shift, si_ref[u, 0])
        j_ref[rows[u], lanes[u]] = _pick_rows(top_c & (PEER_TOPK - 1), si_ref[u, 1])


def _peer_topk(h2, wq_b, keys_b, tq):
    M, D = h2.shape
    nq = 2 * PEER_HEADS
    out_spec = pl.BlockSpec((TOPK_HEADS * PEER_TOPK, tq), lambda i, hd: (hd, i))
    nu = TOPK_HEADS * (tq // TOPK_W)
    np_ = PEER_HEADS * PEER_TOPK
    return pl.pallas_call(
        functools.partial(_peer_topk_kernel, tq=tq),
        grid=(M // tq, PEER_HEADS // TOPK_HEADS),
        in_specs=[pl.BlockSpec((tq, D), lambda i, hd: (i, 0)),
                  pl.BlockSpec(wq_b.shape, lambda i, hd: (0, 0)),
                  pl.BlockSpec(keys_b.shape, lambda i, hd: (0, 0, 0))],
        out_specs=[out_spec, out_spec, out_spec],
        out_shape=[jax.ShapeDtypeStruct((np_, M), I32), jax.ShapeDtypeStruct((np_, M), I32),
                   jax.ShapeDtypeStruct((np_, M), F32)],
        scratch_shapes=[pltpu.VMEM((nq, tq, PEER_HALF), BF16),
                        pltpu.VMEM((nu, 2, PEER_TOPK, TOPK_W), F32),
                        pltpu.VMEM((nu, 2, PEER_TOPK, TOPK_W), I32),
                        pltpu.VMEM((nu, PEER_TOPK, TOPK_W), F32),
                        pltpu.VMEM((nu, PEER_TOPK, TOPK_W), I32)],
        compiler_params=_cparams(("parallel", "arbitrary")),
        name="peer_topk",
    )(h2, wq_b, keys_b)


def _gate_kernel(i_ref, j_ref, g_ref, o_ref, *, tg):
    sub = lax.broadcasted_iota(I32, (PEER_NKEYS, PEER_HEADS * PEER_TOPK), 0)
    rows = o_ref.shape[2]

    def body(t, carry):
        irow = i_ref[pl.ds(t, 1), :]
        jrow = j_ref[pl.ds(t, 1), :]
        grow = g_ref[pl.ds(t, 1), :]
        at = jnp.where(sub == irow, grow, 0.0).astype(BF16)
        bt = jnp.where(sub == jrow, 1.0, 0.0).astype(BF16)
        gt = _dot_nt(at, bt)
        for blk in range(o_ref.shape[0]):
            o_ref[blk, t] = gt[blk * rows:(blk + 1) * rows, :]
        return carry

    lax.fori_loop(0, tg, body, 0, unroll=True)


def _gate_matrix(i_n, j_n, g_n, tg, tt, rows):
    M, P = i_n.shape
    nblk = PEER_NKEYS // rows
    per_tile = tt // tg
    spec = pl.BlockSpec((tg, P), lambda i: (i, 0))
    return pl.pallas_call(
        functools.partial(_gate_kernel, tg=tg),
        grid=(M // tg,),
        in_specs=[spec, spec, spec],
        out_specs=pl.BlockSpec((None, nblk, tg, rows, PEER_NKEYS),
                               lambda i: (i // per_tile, 0, i % per_tile, 0, 0)),
        out_shape=jax.ShapeDtypeStruct((M // tt, nblk, tt, rows, PEER_NKEYS), F32),
        compiler_params=_cparams(("parallel",)),
        name="peer_gates",
    )(i_n, j_n, g_n)


def _peer_dense_kernel(h_ref, u_ref, v_ref, gm_ref, g_ref, b_ref, o_ref, hb_ref, p_ref):
    j = pl.program_id(1)

    @pl.when(j == 0)
    def _():
        hb_ref[...] = h_ref[...].astype(BF16)
        o_ref[...] = jnp.zeros_like(o_ref)

    s = _dot_nt(hb_ref[...], u_ref[...])
    act = 0.5 * s * (1.0 + lax.erf(s * (2.0 ** -0.5)))
    tt = s.shape[0]
    rows = gm_ref.shape[0] // tt
    for r in range(rows):
        sl = slice(r * PEER_NKEYS, (r + 1) * PEER_NKEYS)
        p_ref[:, sl] = (act[:, sl] * gm_ref[pl.ds(r, tt, stride=rows), :]).astype(BF16)
    o_ref[...] += _dot(p_ref[...], v_ref[...])

    @pl.when(j == pl.num_programs(1) - 1)
    def _():
        y = DEEPNORM_ALPHA * h_ref[...] + o_ref[...]
        o_ref[...] = _layer_norm(y, g_ref[...], b_ref[...])


def _peer_dense(h2, u_b, v_b, gmat, ln_g, ln_b, tt, eb):
    M, D = h2.shape
    E = u_b.shape[0]
    rows = eb // PEER_NKEYS
    return pl.pallas_call(
        _peer_dense_kernel,
        grid=(M // tt, E // eb),
        in_specs=[pl.BlockSpec((tt, D), lambda i, j: (i, 0)),
                  pl.BlockSpec((eb, D), lambda i, j: (j, 0)),
                  pl.BlockSpec((eb, D), lambda i, j: (j, 0)),
                  pl.BlockSpec((None, None, tt * rows, PEER_NKEYS), lambda i, j: (i, j, 0, 0)),
                  pl.BlockSpec((1, D), lambda i, j: (0, 0)),
                  pl.BlockSpec((1, D), lambda i, j: (0, 0))],
        out_specs=pl.BlockSpec((tt, D), lambda i, j: (i, 0)),
        out_shape=jax.ShapeDtypeStruct((M, D), F32),
        scratch_shapes=[pltpu.VMEM((tt, D), BF16), pltpu.VMEM((tt, eb), BF16)],
        compiler_params=_cparams(("parallel", "arbitrary")),
        name="peer_dense",
    )(h2, u_b, v_b, gmat, ln_g.reshape(1, -1), ln_b.reshape(1, -1))


def _tile(n, pref):
    t = min(n, pref)
    assert n % t == 0
    return t


def _layer(x, w_in, conv_w, a_log, dt_bias, dn_norm_w, attn_sink, rel_bias, w_out,
           ln1_g, ln1_b, peer_wq, peer_keys, peer_u, peer_v, ln2_g, ln2_b):
    B, S, D = x.shape
    T = B * S
    x2 = x.reshape(T, D)

    proj = _inproj(x2, w_in[:, :COL_TAIL].astype(BF16), _tile(T, 1024), COL_TAIL // 2)
    bg, bgT = _tail(x2, w_in[:, COL_TAIL:], a_log, dt_bias, _tile(T, 1024))
    proj3 = proj.reshape(B, S, -1)

    sink_b = jnp.broadcast_to(attn_sink.astype(F32).reshape(ATTN_KV_HEADS, ATTN_GROUP, 1),
                              (ATTN_KV_HEADS, ATTN_GROUP, 128))
    attn = _attention(proj3, _attn_bias_table(rel_bias), sink_b)

    qkv = _conv(proj3, conv_w, _tile(S, 512))
    o_f, o_b = _deltanet(qkv, bg.reshape(B, S, 128), bgT)

    h = _outproj(attn.reshape(T, -1), o_f.reshape(T, -1), o_b.reshape(T, -1), proj, x2,
                 w_out.astype(BF16), dn_norm_w, ln1_g, ln1_b, _tile(T, 512))

    i_t, j_t, g_t = _peer_topk(h, peer_wq.astype(BF16), peer_keys.astype(BF16), _tile(T, 256))
    tt, eb = _tile(T, 1024), 512
    rows = eb // PEER_NKEYS
    gmat = _gate_matrix(i_t.T, j_t.T, g_t.T, _tile(T, 64), tt, rows)
    gmat = gmat.reshape(T // tt, PEER_NKEYS // rows, tt * rows, PEER_NKEYS)
    y = _peer_dense(h, peer_u.astype(BF16), peer_v.astype(BF16), gmat, ln2_g, ln2_b, tt, eb)
    return y.reshape(B, S, D)


def kernel(x, w_in, conv_w, a_log, dt_bias, dn_norm_w, attn_sink, rel_bias, w_out, ln1_g, ln1_b,
           peer_wq, peer_keys, peer_u, peer_v, ln2_g, ln2_b):
    for l in range(DEPTH):
        x = _layer(x, w_in[l], conv_w[l], a_log[l], dt_bias[l], dn_norm_w[l], attn_sink[l], rel_bias,
                   w_out[l], ln1_g[l], ln1_b[l], peer_wq[l], peer_keys[l], peer_u[l], peer_v[l],
                   ln2_g[l], ln2_b[l])
    return x
```

```python
import functools
import math

import jax
import jax.numpy as jnp
import numpy as np
from jax import lax
from jax.experimental import pallas as pl
from jax.experimental.pallas import tpu as pltpu

F32 = jnp.float32
BF16 = jnp.bfloat16
I32 = jnp.int32

HEAD_DIM = 128
ATTN_HEADS = 8
ATTN_KV_HEADS = 2
ATTN_GROUP = ATTN_HEADS // ATTN_KV_HEADS
WINDOW = 128
ATTN_BLOCK = 128
N_BUCKETS = 32
MAX_DISTANCE = 128
DN_HEADS = 8
DN_DIM = 128
CONV_WIDTH = 5
DN_CHUNK = 128
PEER_HEADS = 8
PEER_NKEYS = 128
PEER_HALF = 128
PEER_TOPK = 16
DEPTH = 1
DEEPNORM_ALPHA = (2.0 * DEPTH) ** 0.25
LN_EPS = 1e-5
RMS_EPS = 1e-6
NEG_BIG = -1e30

COL_AQ = 0
COL_AK = COL_AQ + ATTN_HEADS * HEAD_DIM
COL_AV = COL_AK + ATTN_KV_HEADS * HEAD_DIM
COL_DQ = COL_AV + ATTN_KV_HEADS * HEAD_DIM
COL_DZ = COL_DQ + 3 * DN_HEADS * DN_DIM
COL_TAIL = COL_DZ + DN_HEADS * DN_DIM
N_TAIL = 4 * DN_HEADS

VMEM_LIMIT = 62 * 1024 * 1024


def _cparams(sem):
    return pltpu.CompilerParams(dimension_semantics=sem, vmem_limit_bytes=VMEM_LIMIT)


def _dot(a, b):
    return jnp.dot(a, b, preferred_element_type=F32)


def _dot_nt(a, b, precision=None):
    return lax.dot_general(a, b, (((1,), (1,)), ((), ())), preferred_element_type=F32, precision=precision)


def _inproj_kernel(x_ref, w_ref, wt_ref, wtT_ref, alog_r_ref, dtb_r_ref, alog_c_ref, dtb_c_ref,
                   o_ref, bg_ref, bgT_ref, xb_ref):
    @pl.when(pl.program_id(1) == 0)
    def _():
        xb = x_ref[...].astype(BF16)
        xb_ref[...] = xb
        t = _dot(xb, wt_ref[...])
        tT = _dot_nt(wtT_ref[...], xb)
        lane = lax.broadcasted_iota(I32, t.shape, 1)
        g = -jnp.exp(alog_r_ref[...]) * jax.nn.softplus(t + dtb_r_ref[...])
        bg_ref[...] = jnp.where(lane < 2 * DN_HEADS, jax.nn.sigmoid(t), g)
        sub = lax.broadcasted_iota(I32, tT.shape, 0)
        gT = -jnp.exp(alog_c_ref[...]) * jax.nn.softplus(tT + dtb_c_ref[...])
        bgT_ref[...] = jnp.where(sub < 2 * DN_HEADS, jax.nn.sigmoid(tT), gT)

    o_ref[...] = _dot(xb_ref[...], w_ref[...]).astype(o_ref.dtype)


def _inproj(x2, w_main, w_tail, a_log, dt_bias, tm, tn):
    M, K = x2.shape
    N = w_main.shape[1]
    wt = jnp.zeros((K, 128), BF16).at[:, :N_TAIL].set(w_tail.astype(BF16))
    wtT = w_tail.astype(BF16).T
    al = jnp.zeros((N_TAIL,), F32).at[2 * DN_HEADS:].set(a_log.reshape(-1))
    db = jnp.zeros((N_TAIL,), F32).at[2 * DN_HEADS:].set(dt_bias.reshape(-1))
    al_r = jnp.zeros((1, 128), F32).at[0, :N_TAIL].set(al)
    db_r = jnp.zeros((1, 128), F32).at[0, :N_TAIL].set(db)
    full = lambda shape: pl.BlockSpec(shape, lambda i, j: (0, 0))
    return pl.pallas_call(
        _inproj_kernel,
        grid=(M // tm, N // tn),
        in_specs=[pl.BlockSpec((tm, K), lambda i, j: (i, 0)),
                  pl.BlockSpec((K, tn), lambda i, j: (0, j)),
                  full((K, 128)), full((N_TAIL, K)),
                  full((1, 128)), full((1, 128)), full((N_TAIL, 1)), full((N_TAIL, 1))],
        out_specs=[pl.BlockSpec((tm, tn), lambda i, j: (i, j)),
                   pl.BlockSpec((tm, 128), lambda i, j: (i, 0)),
                   pl.BlockSpec((N_TAIL, tm), lambda i, j: (0, i))],
        out_shape=[jax.ShapeDtypeStruct((M, N), BF16), jax.ShapeDtypeStruct((M, 128), F32),
                   jax.ShapeDtypeStruct((N_TAIL, M), F32)],
        scratch_shapes=[pltpu.VMEM((tm, K), BF16)],
        compiler_params=_cparams(("parallel", "arbitrary")),
        name="inproj",
    )(x2, w_main, wt, wtT, al_r, db_r, al.reshape(N_TAIL, 1), db.reshape(N_TAIL, 1))


def _t5_bucket(rel):
    nb = N_BUCKETS // 2
    max_exact = nb // 2
    ret = jnp.where(rel > 0, nb, 0)
    n = jnp.abs(rel)
    large = max_exact + (jnp.log(jnp.maximum(n, 1).astype(F32) / max_exact)
                         / math.log(MAX_DISTANCE / max_exact) * (nb - max_exact)).astype(I32)
    large = jnp.minimum(large, nb - 1)
    return ret + jnp.where(n < max_exact, n, large)


def _attn_bias_table(rel_bias):
    kj = jnp.arange(3 * ATTN_BLOCK)[:, None]
    qi = jnp.arange(ATTN_BLOCK)[None, :]
    rel = jnp.stack([kj - qi - variant * ATTN_BLOCK for variant in range(3)])
    bucket = jnp.where(jnp.abs(rel) <= WINDOW, _t5_bucket(rel), -1)[:, None]
    tab = jnp.full((3, ATTN_HEADS, 3 * ATTN_BLOCK, ATTN_BLOCK), NEG_BIG, F32)
    rb = rel_bias.astype(F32)
    for bk in range(N_BUCKETS):
        tab = jnp.where(bucket == bk, rb[bk][None, :, None, None], tab)
    return tab


def _attn_kernel(q_ref, k_ref, v_ref, bias_ref, sink_ref, o_ref, *, seq):
    n = pl.program_id(1)
    start = jnp.clip((n - 1) * ATTN_BLOCK, 0, seq - 3 * ATTN_BLOCK)
    start = pl.multiple_of(start, ATTN_BLOCK)
    H = range(ATTN_HEADS)
    hs = [slice(h * HEAD_DIM, (h + 1) * HEAD_DIM) for h in H]
    kvs = [slice(g * HEAD_DIM, (g + 1) * HEAD_DIM) for g in range(ATTN_KV_HEADS)]
    kb = [k_ref[pl.ds(start, 3 * ATTN_BLOCK), sl] for sl in kvs]
    vb = [v_ref[pl.ds(start, 3 * ATTN_BLOCK), sl] for sl in kvs]
    s = [_dot_nt(kb[h // ATTN_GROUP], q_ref[:, hs[h]]) * (HEAD_DIM ** -0.5) + bias_ref[h] for h in H]
    sk = [sink_ref[h // ATTN_GROUP, h % ATTN_GROUP:h % ATTN_GROUP + 1, 0:1] for h in H]
    m = [jnp.maximum(jnp.max(s[h], axis=0, keepdims=True), sk[h]) for h in H]
    p = [jnp.exp(s[h] - m[h]) for h in H]
    den = [jnp.sum(p[h], axis=0, keepdims=True) + jnp.exp(sk[h] - m[h]) for h in H]
    o = [lax.dot_general((p[h] / den[h]).astype(BF16), vb[h // ATTN_GROUP], (((0,), (0,)), ((), ())),
                         preferred_element_type=F32) for h in H]
    for h in H:
        o_ref[:, hs[h]] = o[h].astype(o_ref.dtype)


def _attention(proj3, bias_tab, sink_b):
    B, S, _ = proj3.shape
    nb = S // ATTN_BLOCK
    assert nb >= 3
    qw = ATTN_HEADS * HEAD_DIM
    kvw = ATTN_KV_HEADS * HEAD_DIM
    variant = lambda n: jnp.where(n == 0, 0, jnp.where(n == nb - 1, 2, 1))
    return pl.pallas_call(
        functools.partial(_attn_kernel, seq=S),
        grid=(B, nb),
        in_specs=[pl.BlockSpec((None, ATTN_BLOCK, qw), lambda b, n: (b, n, COL_AQ // qw)),
                  pl.BlockSpec((None, S, kvw), lambda b, n: (b, 0, COL_AK // kvw)),
                  pl.BlockSpec((None, S, kvw), lambda b, n: (b, 0, COL_AV // kvw)),
                  pl.BlockSpec((None, ATTN_HEADS, 3 * ATTN_BLOCK, ATTN_BLOCK),
                               lambda b, n: (variant(n), 0, 0, 0)),
                  pl.BlockSpec(sink_b.shape, lambda b, n: (0, 0, 0))],
        out_specs=pl.BlockSpec((None, ATTN_BLOCK, qw), lambda b, n: (b, n, 0)),
        out_shape=jax.ShapeDtypeStruct((B, S, qw), BF16),
        compiler_params=_cparams(("parallel", "arbitrary")),
        name="window_attn",
    )(proj3, proj3, proj3, bias_tab, sink_b)


CONV_COLS = 4 * DN_DIM
CONV_HALO = 16


def _conv_kernel(x_ref, prev_ref, next_ref, w_ref, o_ref, buf_ref, *, ts, n_sblk):
    s = pl.program_id(1)
    c = pl.program_id(2)
    pad = CONV_WIDTH // 2
    prev = jnp.where(s > 0, prev_ref[...].astype(F32), 0.0)
    nxt = jnp.where(s < n_sblk - 1, next_ref[...].astype(F32), 0.0)
    buf_ref[0:CONV_HALO, :] = prev
    buf_ref[CONV_HALO:CONV_HALO + ts, :] = x_ref[...].astype(F32)
    buf_ref[CONV_HALO + ts:CONV_HALO + ts + CONV_HALO, :] = nxt
    y = jnp.zeros((ts, CONV_COLS), F32)
    for w in range(CONV_WIDTH):
        y = y + buf_ref[CONV_HALO + w - pad:CONV_HALO + w - pad + ts, :] * w_ref[w:w + 1, :]
    y = jax.nn.silu(y)
    is_qk = c < 4
    qscale = jnp.where(c < 2, DN_DIM ** -0.5, 1.0).astype(F32)
    for hh in range(CONV_COLS // DN_DIM):
        yh = y[:, hh * DN_DIM:(hh + 1) * DN_DIM]
        nrm = yh * lax.rsqrt(jnp.sum(yh * yh, axis=-1, keepdims=True) + RMS_EPS) * qscale
        o_ref[:, hh * DN_DIM:(hh + 1) * DN_DIM] = jnp.where(is_qk, nrm, yh)


def _conv(proj3, conv_w, ts):
    B, S, _ = proj3.shape
    n_sblk = S // ts
    c0 = COL_DQ // CONV_COLS
    ncb = 3 * DN_HEADS * DN_DIM // CONV_COLS
    hb = ts // CONV_HALO
    nh = S // CONV_HALO
    return pl.pallas_call(
        functools.partial(_conv_kernel, ts=ts, n_sblk=n_sblk),
        grid=(B, n_sblk, ncb),
        in_specs=[pl.BlockSpec((None, ts, CONV_COLS), lambda b, s, c: (b, s, c0 + c)),
                  pl.BlockSpec((None, CONV_HALO, CONV_COLS),
                               lambda b, s, c: (b, jnp.maximum(s * hb - 1, 0), c0 + c)),
                  pl.BlockSpec((None, CONV_HALO, CONV_COLS),
                               lambda b, s, c: (b, jnp.minimum((s + 1) * hb, nh - 1), c0 + c)),
                  pl.BlockSpec((CONV_WIDTH, CONV_COLS), lambda b, s, c: (0, c))],
        out_specs=pl.BlockSpec((None, ts, CONV_COLS), lambda b, s, c: (b, s, c)),
        out_shape=jax.ShapeDtypeStruct((B, S, 3 * DN_HEADS * DN_DIM), F32),
        scratch_shapes=[pltpu.VMEM((ts + 2 * CONV_HALO, CONV_COLS), F32)],
        compiler_params=_cparams(("parallel", "parallel", "arbitrary")),
        name="short_conv",
    )(proj3, proj3, proj3, conv_w)


def _dn_kernel(qf, kf, vf, bgf, bgTf, qb, kb, vb, bgb, bgTb, of_ref, ob_ref, state_ref):
    @pl.when(pl.program_id(1) == 0)
    def _():
        state_ref[...] = jnp.zeros_like(state_ref)

    C = DN_CHUNK
    hi = lax.Precision.HIGHEST
    row = lax.broadcasted_iota(I32, (C, C), 0)
    col = lax.broadcasted_iota(I32, (C, C), 1)
    eye = jnp.where(row == col, 1.0, 0.0)
    refs = ((qf, kf, vf, bgf, bgTf, of_ref), (qb, kb, vb, bgb, bgTb, ob_ref))
    incl, strict, last, sels, bg, g_cum, g_cum_t = [], [], [], [], [], [], []
    for d in range(2):
        if d == 0:
            inc, stc, lst, lo, up = row >= col, row > col, C - 1, col, row
        else:
            inc, stc, lst, lo, up = row <= col, row < col, 0, row, col
        incl.append(inc)
        strict.append(stc)
        last.append(lst)
        sd, half = [], 1
        while half < C:
            sd.append(((row ^ col) < 2 * half) & ((up & half) != 0) & ((lo & half) == 0))
            half *= 2
        sels.append(sd)
        tri = inc.astype(F32)
        bg.append(refs[d][3][...])
        g_cum.append(jnp.dot(tri, bg[d], preferred_element_type=F32, precision=hi))
        g_cum_t.append(_dot_nt(refs[d][4][...], tri, precision=hi))

    P = [(d, h) for d in range(2) for h in range(DN_HEADS)]
    R = range(len(P))
    hs = [slice(h * DN_DIM, (h + 1) * DN_DIM) for _, h in P]
    q_ref = [refs[d][0] for d, _ in P]
    k_ref = [refs[d][1] for d, _ in P]
    v_ref = [refs[d][2] for d, _ in P]
    o_ref = [refs[d][5] for d, _ in P]
    cb = [d * DN_HEADS + h for d, h in P]
    cg = [2 * DN_HEADS + c for c in cb]
    beta_c = [bg[d][:, cb[i]:cb[i] + 1] for i, (d, _) in enumerate(P)]
    g_c = [g_cum[d][:, cg[i]:cg[i] + 1] for i, (d, _) in enumerate(P)]
    g_r = [g_cum_t[d][cg[i]:cg[i] + 1, :] for i, (d, _) in enumerate(P)]
    decay = [jnp.where(incl[P[i][0]], jnp.exp(jnp.where(incl[P[i][0]], g_c[i] - g_r[i], 0.0)), 0.0) for i in R]
    kbeta = [k_ref[i][:, hs[i]] * beta_c[i] for i in R]
    lmat = [jnp.where(strict[P[i][0]],
                      _dot_nt(kbeta[i].astype(BF16), k_ref[i][:, hs[i]].astype(BF16)) * decay[i], 0.0) for i in R]
    x = [eye - jnp.where(sels[P[i][0]][0], lmat[i], 0.0) for i in R]
    for lvl in range(1, len(sels[0])):
        xb = [x[i].astype(BF16) for i in R]
        t = [_dot(xb[i], jnp.where(sels[P[i][0]][lvl], lmat[i], 0.0).astype(BF16)) for i in R]
        x = [x[i] - _dot(t[i].astype(BF16), xb[i]) for i in R]
    e_g = [jnp.exp(g_c[i]) for i in R]
    uw = [_dot(x[i].astype(BF16),
               jnp.concatenate([v_ref[i][:, hs[i]] * beta_c[i], kbeta[i] * e_g[i]], axis=1).astype(BF16)) for i in R]
    amat = [(_dot_nt(q_ref[i][:, hs[i]].astype(BF16), k_ref[i][:, hs[i]].astype(BF16)) * decay[i]).astype(BF16)
            for i in R]
    g_last = [g_c[i][last[P[i][0]]:last[P[i][0]] + 1, :] for i in R]
    state_b = [state_ref[d, h].astype(BF16) for d, h in P]
    v_new = [(uw[i][:, :DN_DIM] - _dot(uw[i][:, DN_DIM:].astype(BF16), state_b[i])).astype(BF16) for i in R]
    for i in R:
        qg = (q_ref[i][:, hs[i]] * e_g[i]).astype(BF16)
        o_ref[i][:, hs[i]] = _dot(qg, state_b[i]) + _dot(amat[i], v_new[i])
    for i, (d, h) in enumerate(P):
        kd = k_ref[i][:, hs[i]] * jnp.exp(g_last[i] - g_c[i])
        state_ref[d, h] = state_ref[d, h] * jnp.exp(g_last[i]) + _dot(kd.T.astype(BF16), v_new[i])


def _deltanet(qkv, bg3, bgT):
    B, S, _ = qkv.shape
    C = DN_CHUNK
    N = S // C
    hw = DN_HEADS * DN_DIM

    def specs(nmap):
        return [pl.BlockSpec((None, C, hw), lambda b, n: (b, nmap(n), 0)),
                pl.BlockSpec((None, C, hw), lambda b, n: (b, nmap(n), 1)),
                pl.BlockSpec((None, C, hw), lambda b, n: (b, nmap(n), 2)),
                pl.BlockSpec((None, C, 128), lambda b, n: (b, nmap(n), 0)),
                pl.BlockSpec((N_TAIL, C), lambda b, n: (0, b * N + nmap(n)))]

    fwd = lambda n: n
    bwd = lambda n: N - 1 - n
    out_sd = jax.ShapeDtypeStruct((B, S, hw), F32)
    return pl.pallas_call(
        _dn_kernel,
        grid=(B, N),
        in_specs=specs(fwd) + specs(bwd),
        out_specs=[pl.BlockSpec((None, C, hw), lambda b, n: (b, fwd(n), 0)),
                   pl.BlockSpec((None, C, hw), lambda b, n: (b, bwd(n), 0))],
        out_shape=[out_sd, out_sd],
        scratch_shapes=[pltpu.VMEM((2, DN_HEADS, DN_DIM, DN_DIM), F32)],
        compiler_params=_cparams(("parallel", "arbitrary")),
        name="deltanet_scan",
    )(qkv, qkv, qkv, bg3, bgT, qkv, qkv, qkv, bg3, bgT)


def _layer_norm(y, g, b):
    mu = jnp.mean(y, axis=-1, keepdims=True)
    var = jnp.mean(jnp.square(y - mu), axis=-1, keepdims=True)
    return (y - mu) * lax.rsqrt(var + LN_EPS) * g + b


def _outproj_kernel(attn_ref, of_ref, ob_ref, z0_ref, z1_ref, x_ref, w_ref, nw_ref, g_ref, b_ref, h_ref, mix_ref):
    aw = ATTN_HEADS * HEAD_DIM
    mix_ref[:, :aw] = attn_ref[...]
    nw = nw_ref[...]
    half = z0_ref.shape[1] // DN_DIM
    for hh in range(DN_HEADS):
        sl = slice(hh * DN_DIM, (hh + 1) * DN_DIM)
        o = of_ref[:, sl] + ob_ref[:, sl]
        z_ref = z0_ref if hh < half else z1_ref
        zc = (hh % half) * DN_DIM
        z = z_ref[:, zc:zc + DN_DIM].astype(F32)
        o = o * lax.rsqrt(jnp.mean(o * o, axis=-1, keepdims=True) + RMS_EPS) * nw
        mix_ref[:, aw + hh * DN_DIM:aw + (hh + 1) * DN_DIM] = (o * jax.nn.silu(z)).astype(BF16)
    y = DEEPNORM_ALPHA * x_ref[...] + _dot(mix_ref[...], w_ref[...])
    h_ref[...] = _layer_norm(y, g_ref[...], b_ref[...])


def _outproj(attn2, of2, ob2, proj2, x2, w_out_b, norm_w, ln_g, ln_b, tm):
    M, D = x2.shape
    dw = DN_HEADS * DN_DIM
    zb = dw // 2
    z0 = COL_DZ // zb
    row = lambda w: pl.BlockSpec((tm, w), lambda i: (i, 0))
    full = lambda shape: pl.BlockSpec(shape, lambda i: (0, 0))
    return pl.pallas_call(
        _outproj_kernel,
        grid=(M // tm,),
        in_specs=[row(ATTN_HEADS * HEAD_DIM), row(dw), row(dw),
                  pl.BlockSpec((tm, zb), lambda i: (i, z0)), pl.BlockSpec((tm, zb), lambda i: (i, z0 + 1)),
                  row(D), full(w_out_b.shape), full((1, DN_DIM)), full((1, D)), full((1, D))],
        out_specs=row(D),
        out_shape=jax.ShapeDtypeStruct((M, D), F32),
        scratch_shapes=[pltpu.VMEM((tm, ATTN_HEADS * HEAD_DIM + dw), BF16)],
        compiler_params=_cparams(("parallel",)),
        name="outproj_ln",
    )(attn2, of2, ob2, proj2, proj2, x2, w_out_b, norm_w.reshape(1, -1), ln_g.reshape(1, -1), ln_b.reshape(1, -1))


TOPK_W = 128
TOPK_HEADS = 2


def _topk_rounds(ss, labels, val_refs, idx_refs):
    big = jnp.iinfo(jnp.int32).max
    for r in range(PEER_TOPK):
        ms = [jnp.max(s, axis=0, keepdims=True) for s in ss]
        idxs = [jnp.min(jnp.where(s == m, lab, big), axis=0, keepdims=True) for s, m, lab in zip(ss, ms, labels)]
        for n in range(len(ss)):
            val_refs[n][r:r + 1, :] = ms[n]
            idx_refs[n][r:r + 1, :] = idxs[n]
        ss = [jnp.where(lab == idx, -jnp.inf, s) for s, idx, lab in zip(ss, idxs, labels)]


def _staircase_candidates(sv1, sv2):
    w = sv1.shape[1]
    iota8 = lax.broadcasted_iota(I32, (8, w), 0)
    vals, labs = [], []
    for a in range(8):
        nb = PEER_TOPK // (a + 1)
        for b0 in range(0, nb, 8):
            v = sv1[a:a + 1, :] + sv2[b0:b0 + 8, :]
            if nb - b0 < 8:
                v = jnp.where(iota8 < nb - b0, v, -jnp.inf)
            vals.append(v)
            labs.append(iota8 + (a * PEER_TOPK + b0))
    vals.append(sv1[8:16, :] + sv2[0:1, :])
    labs.append((iota8 + 8) * PEER_TOPK)
    return jnp.concatenate(vals, axis=0), jnp.concatenate(labs, axis=0)


def _pick_rows(sel, table):
    out = jnp.zeros(sel.shape, table.dtype)
    for a in range(PEER_TOPK):
        out = jnp.where(sel == a, table[a:a + 1, :], out)
    return out


def _peer_topk_kernel(h_ref, wq_ref, keys_ref, i_ref, j_ref, g_ref,
                      qs_ref, sv_ref, si_ref, tv_ref, tc_ref, *, tq):
    hd = pl.program_id(1)
    nq = 2 * PEER_HEADS

    @pl.when(hd == 0)
    def _():
        q = _dot(h_ref[...].astype(BF16), wq_ref[...])
        for c in range(nq):
            qs_ref[c] = q[:, c * PEER_HALF:(c + 1) * PEER_HALF].astype(BF16)

    units = [(hh, sl) for hh in range(TOPK_HEADS) for sl in range(tq // TOPK_W)]
    U = range(len(units))
    lanes = [slice(sl * TOPK_W, (sl + 1) * TOPK_W) for _, sl in units]
    rows = [slice(hh * PEER_TOPK, (hh + 1) * PEER_TOPK) for hh, _ in units]
    key_iota = lax.broadcasted_iota(I32, (PEER_NKEYS, TOPK_W), 0)
    probs = [(u, p) for u in U for p in range(2)]
    scores = [_dot_nt(keys_ref[p], qs_ref[2 * (TOPK_HEADS * hd + units[u][0]) + p, lanes[u], :])
              for u, p in probs]
    _topk_rounds(scores, [key_iota] * len(probs),
                 [sv_ref.at[u, p] for u, p in probs], [si_ref.at[u, p] for u, p in probs])
    cands = [_staircase_candidates(sv_ref[u, 0], sv_ref[u, 1]) for u in U]
    _topk_rounds([c[0] for c in cands], [c[1] for c in cands],
                 [tv_ref.at[u] for u in U], [tc_ref.at[u] for u in U])
    shift = int(math.log2(PEER_TOPK))
    for u in U:
        top_s = tv_ref[u]
        top_c = tc_ref[u]
        e = jnp.exp(top_s - top_s[0:1, :])
        g_ref[rows[u], lanes[u]] = e / jnp.sum(e, axis=0, keepdims=True)
        i_ref[rows[u], lanes[u]] = _pick_rows(top_c >> shift, si_ref[u, 0])
        j_ref[rows[u], lanes[u]] = _pick_rows(top_c & (PEER_TOPK - 1), si_ref[u, 1])


def _peer_topk(h2, wq_b, keys_b, tq):
    M, D = h2.shape
    nq = 2 * PEER_HEADS
    out_spec = pl.BlockSpec((TOPK_HEADS * PEER_TOPK, tq), lambda i, hd: (hd, i))
    nu = TOPK_HEADS * (tq // TOPK_W)
    np_ = PEER_HEADS * PEER_TOPK
    return pl.pallas_call(
        functools.partial(_peer_topk_kernel, tq=tq),
        grid=(M // tq, PEER_HEADS // TOPK_HEADS),
        in_specs=[pl.BlockSpec((tq, D), lambda i, hd: (i, 0)),
                  pl.BlockSpec(wq_b.shape, lambda i, hd: (0, 0)),
                  pl.BlockSpec(keys_b.shape, lambda i, hd: (0, 0, 0))],
        out_specs=[out_spec, out_spec, out_spec],
        out_shape=[jax.ShapeDtypeStruct((np_, M), I32), jax.ShapeDtypeStruct((np_, M), I32),
                   jax.ShapeDtypeStruct((np_, M), F32)],
        scratch_shapes=[pltpu.VMEM((nq, tq, PEER_HALF), BF16),
                        pltpu.VMEM((nu, 2, PEER_TOPK, TOPK_W), F32),
                        pltpu.VMEM((nu, 2, PEER_TOPK, TOPK_W), I32),
                        pltpu.VMEM((nu, PEER_TOPK, TOPK_W), F32),
                        pltpu.VMEM((nu, PEER_TOPK, TOPK_W), I32)],
        compiler_params=_cparams(("parallel", "arbitrary")),
        name="peer_topk",
    )(h2, wq_b, keys_b)


def _gate_kernel(i_ref, j_ref, g_ref, o_ref, *, tg):
    sub = lax.broadcasted_iota(I32, (PEER_NKEYS, PEER_HEADS * PEER_TOPK), 0)
    rows = o_ref.shape[2]

    def body(t, carry):
        irow = i_ref[pl.ds(t, 1), :]
        jrow = j_ref[pl.ds(t, 1), :]
        grow = g_ref[pl.ds(t, 1), :]
        at = jnp.where(sub == irow, grow, 0.0).astype(BF16)
        bt = jnp.where(sub == jrow, 1.0, 0.0).astype(BF16)
        gt = _dot_nt(at, bt)
        for blk in range(o_ref.shape[0]):
            o_ref[blk, t] = gt[blk * rows:(blk + 1) * rows, :]
        return carry

    lax.fori_loop(0, tg, body, 0, unroll=True)


def _gate_matrix(i_n, j_n, g_n, tg, tt, rows):
    M, P = i_n.shape
    nblk = PEER_NKEYS // rows
    per_tile = tt // tg
    spec = pl.BlockSpec((tg, P), lambda i: (i, 0))
    return pl.pallas_call(
        functools.partial(_gate_kernel, tg=tg),
        grid=(M // tg,),
        in_specs=[spec, spec, spec],
        out_specs=pl.BlockSpec((None, nblk, tg, rows, PEER_NKEYS),
                               lambda i: (i // per_tile, 0, i % per_tile, 0, 0)),
        out_shape=jax.ShapeDtypeStruct((M // tt, nblk, tt, rows, PEER_NKEYS), F32),
        compiler_params=_cparams(("parallel",)),
        name="peer_gates",
    )(i_n, j_n, g_n)


def _peer_dense_kernel(h_ref, u_ref, v_ref, gm_ref, g_ref, b_ref, o_ref, hb_ref, p_ref):
    j = pl.program_id(1)

    @pl.when(j == 0)
    def _():
        hb_ref[...] = h_ref[...].astype(BF16)
        o_ref[...] = jnp.zeros_like(o_ref)

    s = _dot_nt(hb_ref[...], u_ref[...])
    act = 0.5 * s * (1.0 + lax.erf(s * (2.0 ** -0.5)))
    tt = s.shape[0]
    rows = gm_ref.shape[0] // tt
    for r in range(rows):
        sl = slice(r * PEER_NKEYS, (r + 1) * PEER_NKEYS)
        p_ref[:, sl] = (act[:, sl] * gm_ref[pl.ds(r, tt, stride=rows), :]).astype(BF16)
    o_ref[...] += _dot(p_ref[...], v_ref[...])

    @pl.when(j == pl.num_programs(1) - 1)
    def _():
        y = DEEPNORM_ALPHA * h_ref[...] + o_ref[...]
        o_ref[...] = _layer_norm(y, g_ref[...], b_ref[...])


def _peer_dense(h2, u_b, v_b, gmat, ln_g, ln_b, tt, eb):
    M, D = h2.shape
    E = u_b.shape[0]
    rows = eb // PEER_NKEYS
    return pl.pallas_call(
        _peer_dense_kernel,
        grid=(M // tt, E // eb),
        in_specs=[pl.BlockSpec((tt, D), lambda i, j: (i, 0)),
                  pl.BlockSpec((eb, D), lambda i, j: (j, 0)),
                  pl.BlockSpec((eb, D), lambda i, j: (j, 0)),
                  pl.BlockSpec((None, None, tt * rows, PEER_NKEYS), lambda i, j: (i, j, 0, 0)),
                  pl.BlockSpec((1, D), lambda i, j: (0, 0)),
                  pl.BlockSpec((1, D), lambda i, j: (0, 0))],
        out_specs=pl.BlockSpec((tt, D), lambda i, j: (i, 0)),
        out_shape=jax.ShapeDtypeStruct((M, D), F32),
        scratch_shapes=[pltpu.VMEM((tt, D), BF16), pltpu.VMEM((tt, eb), BF16)],
        compiler_params=_cparams(("parallel", "arbitrary")),
        name="peer_dense",
    )(h2, u_b, v_b, gmat, ln_g.reshape(1, -1), ln_b.reshape(1, -1))


def _tile(n, pref):
    t = min(n, pref)
    assert n % t == 0
    return t


def _layer(x, w_in, conv_w, a_log, dt_bias, dn_norm_w, attn_sink, rel_bias, w_out,
           ln1_g, ln1_b, peer_wq, peer_keys, peer_u, peer_v, ln2_g, ln2_b):
    B, S, D = x.shape
    T = B * S
    x2 = x.reshape(T, D)

    proj, bg, bgT = _inproj(x2, w_in[:, :COL_TAIL].astype(BF16), w_in[:, COL_TAIL:], a_log, dt_bias,
                            _tile(T, 1024), COL_TAIL // 2)
    proj3 = proj.reshape(B, S, -1)

    sink_b = jnp.broadcast_to(attn_sink.astype(F32).reshape(ATTN_KV_HEADS, ATTN_GROUP, 1),
                              (ATTN_KV_HEADS, ATTN_GROUP, 128))
    attn = _attention(proj3, _attn_bias_table(rel_bias), sink_b)

    qkv = _conv(proj3, conv_w, _tile(S, 512))
    o_f, o_b = _deltanet(qkv, bg.reshape(B, S, 128), bgT)

    h = _outproj(attn.reshape(T, -1), o_f.reshape(T, -1), o_b.reshape(T, -1), proj, x2,
                 w_out.astype(BF16), dn_norm_w, ln1_g, ln1_b, _tile(T, 512))

    i_t, j_t, g_t = _peer_topk(h, peer_wq.astype(BF16), peer_keys.astype(BF16), _tile(T, 256))
    tt, eb = _tile(T, 1024), 512
    rows = eb // PEER_NKEYS
    gmat = _gate_matrix(i_t.T, j_t.T, g_t.T, _tile(T, 64), tt, rows)
    gmat = gmat.reshape(T // tt, PEER_NKEYS // rows, tt * rows, PEER_NKEYS)
    y = _peer_dense(h, peer_u.astype(BF16), peer_v.astype(BF16), gmat, ln2_g, ln2_b, tt, eb)
    return y.reshape(B, S, D)


def kernel(x, w_in, conv_w, a_log, dt_bias, dn_norm_w, attn_sink, rel_bias, w_out, ln1_g, ln1_b,
           peer_wq, peer_keys, peer_u, peer_v, ln2_g, ln2_b):
    for l in range(DEPTH):
        x = _layer(x, w_in[l], conv_w[l], a_log[l], dt_bias[l], dn_norm_w[l], attn_sink[l], rel_bias,
                   w_out[l], ln1_g[l], ln1_b[l], peer_wq[l], peer_keys[l], peer_u[l], peer_v[l],
                   ln2_g[l], ln2_b[l])
    return x
```

```python
import functools
import math

import jax
import jax.numpy as jnp
import numpy as np
from jax import lax
from jax.experimental import pallas as pl
from jax.experimental.pallas import tpu as pltpu

F32 = jnp.float32
BF16 = jnp.bfloat16
I32 = jnp.int32

HEAD_DIM = 128
ATTN_HEADS = 8
ATTN_KV_HEADS = 2
ATTN_GROUP = ATTN_HEADS // ATTN_KV_HEADS
WINDOW = 128
ATTN_BLOCK = 128
N_BUCKETS = 32
MAX_DISTANCE = 128
DN_HEADS = 8
DN_DIM = 128
CONV_WIDTH = 5
DN_CHUNK = 128
PEER_HEADS = 8
PEER_NKEYS = 128
PEER_HALF = 128
PEER_TOPK = 16
DEPTH = 1
DEEPNORM_ALPHA = (2.0 * DEPTH) ** 0.25
LN_EPS = 1e-5
RMS_EPS = 1e-6
NEG_BIG = -1e30

COL_AQ = 0
COL_AK = COL_AQ + ATTN_HEADS * HEAD_DIM
COL_AV = COL_AK + ATTN_KV_HEADS * HEAD_DIM
COL_DQ = COL_AV + ATTN_KV_HEADS * HEAD_DIM
COL_DZ = COL_DQ + 3 * DN_HEADS * DN_DIM
COL_TAIL = COL_DZ + DN_HEADS * DN_DIM
N_TAIL = 4 * DN_HEADS

VMEM_LIMIT = 62 * 1024 * 1024


def _cparams(sem):
    return pltpu.CompilerParams(dimension_semantics=sem, vmem_limit_bytes=VMEM_LIMIT)


def _dot(a, b):
    return jnp.dot(a, b, preferred_element_type=F32)


def _dot_nt(a, b, precision=None):
    return lax.dot_general(a, b, (((1,), (1,)), ((), ())), preferred_element_type=F32, precision=precision)


def _inproj_kernel(x_ref, w_ref, wt_ref, wtT_ref, alog_r_ref, dtb_r_ref, alog_c_ref, dtb_c_ref,
                   o_ref, bg_ref, bgT_ref, xb_ref):
    @pl.when(pl.program_id(1) == 0)
    def _():
        xb = x_ref[...].astype(BF16)
        xb_ref[...] = xb
        t = _dot(xb, wt_ref[...])
        tT = _dot_nt(wtT_ref[...], xb)
        lane = lax.broadcasted_iota(I32, t.shape, 1)
        g = -jnp.exp(alog_r_ref[...]) * jax.nn.softplus(t + dtb_r_ref[...])
        bg_ref[...] = jnp.where(lane < 2 * DN_HEADS, jax.nn.sigmoid(t), g)
        sub = lax.broadcasted_iota(I32, tT.shape, 0)
        gT = -jnp.exp(alog_c_ref[...]) * jax.nn.softplus(tT + dtb_c_ref[...])
        bgT_ref[...] = jnp.where(sub < 2 * DN_HEADS, jax.nn.sigmoid(tT), gT)

    o_ref[...] = _dot(xb_ref[...], w_ref[...]).astype(o_ref.dtype)


def _inproj(x2, w_main, w_tail, a_log, dt_bias, tm, tn):
    M, K = x2.shape
    N = w_main.shape[1]
    wt = jnp.zeros((K, 128), BF16).at[:, :N_TAIL].set(w_tail.astype(BF16))
    wtT = w_tail.astype(BF16).T
    al = jnp.zeros((N_TAIL,), F32).at[2 * DN_HEADS:].set(a_log.reshape(-1))
    db = jnp.zeros((N_TAIL,), F32).at[2 * DN_HEADS:].set(dt_bias.reshape(-1))
    al_r = jnp.zeros((1, 128), F32).at[0, :N_TAIL].set(al)
    db_r = jnp.zeros((1, 128), F32).at[0, :N_TAIL].set(db)
    full = lambda shape: pl.BlockSpec(shape, lambda i, j: (0, 0))
    return pl.pallas_call(
        _inproj_kernel,
        grid=(M // tm, N // tn),
        in_specs=[pl.BlockSpec((tm, K), lambda i, j: (i, 0)),
                  pl.BlockSpec((K, tn), lambda i, j: (0, j)),
                  full((K, 128)), full((N_TAIL, K)),
                  full((1, 128)), full((1, 128)), full((N_TAIL, 1)), full((N_TAIL, 1))],
        out_specs=[pl.BlockSpec((tm, tn), lambda i, j: (i, j)),
                   pl.BlockSpec((tm, 128), lambda i, j: (i, 0)),
                   pl.BlockSpec((N_TAIL, tm), lambda i, j: (0, i))],
        out_shape=[jax.ShapeDtypeStruct((M, N), BF16), jax.ShapeDtypeStruct((M, 128), F32),
                   jax.ShapeDtypeStruct((N_TAIL, M), F32)],
        scratch_shapes=[pltpu.VMEM((tm, K), BF16)],
        compiler_params=_cparams(("parallel", "arbitrary")),
        name="inproj",
    )(x2, w_main, wt, wtT, al_r, db_r, al.reshape(N_TAIL, 1), db.reshape(N_TAIL, 1))


def _t5_bucket(rel):
    nb = N_BUCKETS // 2
    max_exact = nb // 2
    ret = jnp.where(rel > 0, nb, 0)
    n = jnp.abs(rel)
    large = max_exact + (jnp.log(jnp.maximum(n, 1).astype(F32) / max_exact)
                         / math.log(MAX_DISTANCE / max_exact) * (nb - max_exact)).astype(I32)
    large = jnp.minimum(large, nb - 1)
    return ret + jnp.where(n < max_exact, n, large)


def _attn_bias_table(rel_bias):
    kj = jnp.arange(3 * ATTN_BLOCK)[:, None]
    qi = jnp.arange(ATTN_BLOCK)[None, :]
    rel = jnp.stack([kj - qi - variant * ATTN_BLOCK for variant in range(3)])
    bucket = jnp.where(jnp.abs(rel) <= WINDOW, _t5_bucket(rel), -1)[:, None]
    tab = jnp.full((3, ATTN_HEADS, 3 * ATTN_BLOCK, ATTN_BLOCK), NEG_BIG, F32)
    rb = rel_bias.astype(F32)
    for bk in range(N_BUCKETS):
        tab = jnp.where(bucket == bk, rb[bk][None, :, None, None], tab)
    return tab


def _attn_kernel(q_ref, k_ref, v_ref, bias_ref, sink_ref, o_ref, *, seq):
    n = pl.program_id(1)
    start = jnp.clip((n - 1) * ATTN_BLOCK, 0, seq - 3 * ATTN_BLOCK)
    start = pl.multiple_of(start, ATTN_BLOCK)
    H = range(ATTN_HEADS)
    hs = [slice(h * HEAD_DIM, (h + 1) * HEAD_DIM) for h in H]
    kvs = [slice(g * HEAD_DIM, (g + 1) * HEAD_DIM) for g in range(ATTN_KV_HEADS)]
    kb = [k_ref[pl.ds(start, 3 * ATTN_BLOCK), sl] for sl in kvs]
    vb = [v_ref[pl.ds(start, 3 * ATTN_BLOCK), sl] for sl in kvs]
    s = [_dot_nt(kb[h // ATTN_GROUP], q_ref[:, hs[h]]) * (HEAD_DIM ** -0.5) + bias_ref[h] for h in H]
    sk = [sink_ref[h // ATTN_GROUP, h % ATTN_GROUP:h % ATTN_GROUP + 1, 0:1] for h in H]
    m = [jnp.maximum(jnp.max(s[h], axis=0, keepdims=True), sk[h]) for h in H]
    p = [jnp.exp(s[h] - m[h]) for h in H]
    den = [jnp.sum(p[h], axis=0, keepdims=True) + jnp.exp(sk[h] - m[h]) for h in H]
    o = [lax.dot_general((p[h] / den[h]).astype(BF16), vb[h // ATTN_GROUP], (((0,), (0,)), ((), ())),
                         preferred_element_type=F32) for h in H]
    for h in H:
        o_ref[:, hs[h]] = o[h].astype(o_ref.dtype)


def _attention(proj3, bias_tab, sink_b):
    B, S, _ = proj3.shape
    nb = S // ATTN_BLOCK
    assert nb >= 3
    qw = ATTN_HEADS * HEAD_DIM
    kvw = ATTN_KV_HEADS * HEAD_DIM
    variant = lambda n: jnp.where(n == 0, 0, jnp.where(n == nb - 1, 2, 1))
    return pl.pallas_call(
        functools.partial(_attn_kernel, seq=S),
        grid=(B, nb),
        in_specs=[pl.BlockSpec((None, ATTN_BLOCK, qw), lambda b, n: (b, n, COL_AQ // qw)),
                  pl.BlockSpec((None, S, kvw), lambda b, n: (b, 0, COL_AK // kvw)),
                  pl.BlockSpec((None, S, kvw), lambda b, n: (b, 0, COL_AV // kvw)),
                  pl.BlockSpec((None, ATTN_HEADS, 3 * ATTN_BLOCK, ATTN_BLOCK),
                               lambda b, n: (variant(n), 0, 0, 0)),
                  pl.BlockSpec(sink_b.shape, lambda b, n: (0, 0, 0))],
        out_specs=pl.BlockSpec((None, ATTN_BLOCK, qw), lambda b, n: (b, n, 0)),
        out_shape=jax.ShapeDtypeStruct((B, S, qw), BF16),
        compiler_params=_cparams(("parallel", "arbitrary")),
        name="window_attn",
    )(proj3, proj3, proj3, bias_tab, sink_b)


CONV_COLS = 4 * DN_DIM
CONV_HALO = 16


def _conv_kernel(x_ref, prev_ref, next_ref, w_ref, o_ref, buf_ref, *, ts, n_sblk):
    s = pl.program_id(1)
    c = pl.program_id(2)
    pad = CONV_WIDTH // 2
    prev = jnp.where(s > 0, prev_ref[...].astype(F32), 0.0)
    nxt = jnp.where(s < n_sblk - 1, next_ref[...].astype(F32), 0.0)
    buf_ref[0:CONV_HALO, :] = prev
    buf_ref[CONV_HALO:CONV_HALO + ts, :] = x_ref[...].astype(F32)
    buf_ref[CONV_HALO + ts:CONV_HALO + ts + CONV_HALO, :] = nxt
    y = jnp.zeros((ts, CONV_COLS), F32)
    for w in range(CONV_WIDTH):
        y = y + buf_ref[CONV_HALO + w - pad:CONV_HALO + w - pad + ts, :] * w_ref[w:w + 1, :]
    y = jax.nn.silu(y)
    is_qk = c < 4
    qscale = jnp.where(c < 2, DN_DIM ** -0.5, 1.0).astype(F32)
    for hh in range(CONV_COLS // DN_DIM):
        yh = y[:, hh * DN_DIM:(hh + 1) * DN_DIM]
        nrm = yh * lax.rsqrt(jnp.sum(yh * yh, axis=-1, keepdims=True) + RMS_EPS) * qscale
        o_ref[:, hh * DN_DIM:(hh + 1) * DN_DIM] = jnp.where(is_qk, nrm, yh)


def _conv(proj3, conv_w, ts):
    B, S, _ = proj3.shape
    n_sblk = S // ts
    c0 = COL_DQ // CONV_COLS
    ncb = 3 * DN_HEADS * DN_DIM // CONV_COLS
    hb = ts // CONV_HALO
    nh = S // CONV_HALO
    return pl.pallas_call(
        functools.partial(_conv_kernel, ts=ts, n_sblk=n_sblk),
        grid=(B, n_sblk, ncb),
        in_specs=[pl.BlockSpec((None, ts, CONV_COLS), lambda b, s, c: (b, s, c0 + c)),
                  pl.BlockSpec((None, CONV_HALO, CONV_COLS),
                               lambda b, s, c: (b, jnp.maximum(s * hb - 1, 0), c0 + c)),
                  pl.BlockSpec((None, CONV_HALO, CONV_COLS),
                               lambda b, s, c: (b, jnp.minimum((s + 1) * hb, nh - 1), c0 + c)),
                  pl.BlockSpec((CONV_WIDTH, CONV_COLS), lambda b, s, c: (0, c))],
        out_specs=pl.BlockSpec((None, ts, CONV_COLS), lambda b, s, c: (b, s, c)),
        out_shape=jax.ShapeDtypeStruct((B, S, 3 * DN_HEADS * DN_DIM), F32),
        scratch_shapes=[pltpu.VMEM((ts + 2 * CONV_HALO, CONV_COLS), F32)],
        compiler_params=_cparams(("parallel", "parallel", "arbitrary")),
        name="short_conv",
    )(proj3, proj3, proj3, conv_w)


def _dn_kernel(qf, kf, vf, bgf, bgTf, qb, kb, vb, bgb, bgTb, of_ref, ob_ref, state_ref):
    @pl.when(pl.program_id(1) == 0)
    def _():
        state_ref[...] = jnp.zeros_like(state_ref)

    C = DN_CHUNK
    hi = lax.Precision.HIGHEST
    row = lax.broadcasted_iota(I32, (C, C), 0)
    col = lax.broadcasted_iota(I32, (C, C), 1)
    eye = jnp.where(row == col, 1.0, 0.0)
    refs = ((qf, kf, vf, bgf, bgTf, of_ref), (qb, kb, vb, bgb, bgTb, ob_ref))
    incl, strict, last, sels, bg, g_cum, g_cum_t = [], [], [], [], [], [], []
    for d in range(2):
        if d == 0:
            inc, stc, lst, lo, up = row >= col, row > col, C - 1, col, row
        else:
            inc, stc, lst, lo, up = row <= col, row < col, 0, row, col
        incl.append(inc)
        strict.append(stc)
        last.append(lst)
        sd, half = [], 1
        while half < C:
            sd.append(((row ^ col) < 2 * half) & ((up & half) != 0) & ((lo & half) == 0))
            half *= 2
        sels.append(sd)
        tri = inc.astype(F32)
        bg.append(refs[d][3][...])
        g_cum.append(jnp.dot(tri, bg[d], preferred_element_type=F32, precision=hi))
        g_cum_t.append(_dot_nt(refs[d][4][...], tri, precision=hi))

    P = [(d, h) for d in range(2) for h in range(DN_HEADS)]
    R = range(len(P))
    hs = [slice(h * DN_DIM, (h + 1) * DN_DIM) for _, h in P]
    q_ref = [refs[d][0] for d, _ in P]
    k_ref = [refs[d][1] for d, _ in P]
    v_ref = [refs[d][2] for d, _ in P]
    o_ref = [refs[d][5] for d, _ in P]
    cb = [d * DN_HEADS + h for d, h in P]
    cg = [2 * DN_HEADS + c for c in cb]
    beta_c = [bg[d][:, cb[i]:cb[i] + 1] for i, (d, _) in enumerate(P)]
    g_c = [g_cum[d][:, cg[i]:cg[i] + 1] for i, (d, _) in enumerate(P)]
    g_r = [g_cum_t[d][cg[i]:cg[i] + 1, :] for i, (d, _) in enumerate(P)]
    decay = [jnp.where(incl[P[i][0]], jnp.exp(jnp.where(incl[P[i][0]], g_c[i] - g_r[i], 0.0)), 0.0) for i in R]
    kbeta = [k_ref[i][:, hs[i]] * beta_c[i] for i in R]
    lmat = [jnp.where(strict[P[i][0]],
                      _dot_nt(kbeta[i].astype(BF16), k_ref[i][:, hs[i]].astype(BF16)) * decay[i], 0.0) for i in R]
    x = [eye - jnp.where(sels[P[i][0]][0], lmat[i], 0.0) for i in R]
    for lvl in range(1, len(sels[0])):
        xb = [x[i].astype(BF16) for i in R]
        t = [_dot(xb[i], jnp.where(sels[P[i][0]][lvl], lmat[i], 0.0).astype(BF16)) for i in R]
        x = [x[i] - _dot(t[i].astype(BF16), xb[i]) for i in R]
    e_g = [jnp.exp(g_c[i]) for i in R]
    uw = [_dot(x[i].astype(BF16),
               jnp.concatenate([v_ref[i][:, hs[i]] * beta_c[i], kbeta[i] * e_g[i]], axis=1).astype(BF16)) for i in R]
    amat = [(_dot_nt(q_ref[i][:, hs[i]].astype(BF16), k_ref[i][:, hs[i]].astype(BF16)) * decay[i]).astype(BF16)
            for i in R]
    g_last = [g_c[i][last[P[i][0]]:last[P[i][0]] + 1, :] for i in R]
    state_b = [state_ref[d, h].astype(BF16) for d, h in P]
    v_new = [(uw[i][:, :DN_DIM] - _dot(uw[i][:, DN_DIM:].astype(BF16), state_b[i])).astype(BF16) for i in R]
    for i in R:
        qg = (q_ref[i][:, hs[i]] * e_g[i]).astype(BF16)
        o_ref[i][:, hs[i]] = _dot(qg, state_b[i]) + _dot(amat[i], v_new[i])
    for i, (d, h) in enumerate(P):
        kd = k_ref[i][:, hs[i]] * jnp.exp(g_last[i] - g_c[i])
        state_ref[d, h] = state_ref[d, h] * jnp.exp(g_last[i]) + _dot(kd.T.astype(BF16), v_new[i])


def _deltanet(qkv, bg3, bgT):
    B, S, _ = qkv.shape
    C = DN_CHUNK
    N = S // C
    hw = DN_HEADS * DN_DIM

    def specs(nmap):
        return [pl.BlockSpec((None, C, hw), lambda b, n: (b, nmap(n), 0)),
                pl.BlockSpec((None, C, hw), lambda b, n: (b, nmap(n), 1)),
                pl.BlockSpec((None, C, hw), lambda b, n: (b, nmap(n), 2)),
                pl.BlockSpec((None, C, 128), lambda b, n: (b, nmap(n), 0)),
                pl.BlockSpec((N_TAIL, C), lambda b, n: (0, b * N + nmap(n)))]

    fwd = lambda n: n
    bwd = lambda n: N - 1 - n
    out_sd = jax.ShapeDtypeStruct((B, S, hw), F32)
    return pl.pallas_call(
        _dn_kernel,
        grid=(B, N),
        in_specs=specs(fwd) + specs(bwd),
        out_specs=[pl.BlockSpec((None, C, hw), lambda b, n: (b, fwd(n), 0)),
                   pl.BlockSpec((None, C, hw), lambda b, n: (b, bwd(n), 0))],
        out_shape=[out_sd, out_sd],
        scratch_shapes=[pltpu.VMEM((2, DN_HEADS, DN_DIM, DN_DIM), F32)],
        compiler_params=_cparams(("parallel", "arbitrary")),
        name="deltanet_scan",
    )(qkv, qkv, qkv, bg3, bgT, qkv, qkv, qkv, bg3, bgT)


def _layer_norm(y, g, b):
    mu = jnp.mean(y, axis=-1, keepdims=True)
    var = jnp.mean(jnp.square(y - mu), axis=-1, keepdims=True)
    return (y - mu) * lax.rsqrt(var + LN_EPS) * g + b


def _outproj_kernel(attn_ref, of_ref, ob_ref, z0_ref, z1_ref, x_ref, w_ref, nw_ref, g_ref, b_ref, h_ref, mix_ref):
    aw = ATTN_HEADS * HEAD_DIM
    mix_ref[:, :aw] = attn_ref[...]
    nw = nw_ref[...]
    half = z0_ref.shape[1] // DN_DIM
    for hh in range(DN_HEADS):
        sl = slice(hh * DN_DIM, (hh + 1) * DN_DIM)
        o = of_ref[:, sl] + ob_ref[:, sl]
        z_ref = z0_ref if hh < half else z1_ref
        zc = (hh % half) * DN_DIM
        z = z_ref[:, zc:zc + DN_DIM].astype(F32)
        o = o * lax.rsqrt(jnp.mean(o * o, axis=-1, keepdims=True) + RMS_EPS) * nw
        mix_ref[:, aw + hh * DN_DIM:aw + (hh + 1) * DN_DIM] = (o * jax.nn.silu(z)).astype(BF16)
    y = DEEPNORM_ALPHA * x_ref[...] + _dot(mix_ref[...], w_ref[...])
    h_ref[...] = _layer_norm(y, g_ref[...], b_ref[...])


def _outproj(attn2, of2, ob2, proj2, x2, w_out_b, norm_w, ln_g, ln_b, tm):
    M, D = x2.shape
    dw = DN_HEADS * DN_DIM
    zb = dw // 2
    z0 = COL_DZ // zb
    row = lambda w: pl.BlockSpec((tm, w), lambda i: (i, 0))
    full = lambda shape: pl.BlockSpec(shape, lambda i: (0, 0))
    return pl.pallas_call(
        _outproj_kernel,
        grid=(M // tm,),
        in_specs=[row(ATTN_HEADS * HEAD_DIM), row(dw), row(dw),
                  pl.BlockSpec((tm, zb), lambda i: (i, z0)), pl.BlockSpec((tm, zb), lambda i: (i, z0 + 1)),
                  row(D), full(w_out_b.shape), full((1, DN_DIM)), full((1, D)), full((1, D))],
        out_specs=row(D),
        out_shape=jax.ShapeDtypeStruct((M, D), F32),
        scratch_shapes=[pltpu.VMEM((tm, ATTN_HEADS * HEAD_DIM + dw), BF16)],
        compiler_params=_cparams(("parallel",)),
        name="outproj_ln",
    )(attn2, of2, ob2, proj2, proj2, x2, w_out_b, norm_w.reshape(1, -1), ln_g.reshape(1, -1), ln_b.reshape(1, -1))


TOPK_W = 128
TOPK_HEADS = 4


def _tree_reduce(fn, xs):
    xs = list(xs)
    while len(xs) > 1:
        xs = [fn(xs[i], xs[i + 1]) if i + 1 < len(xs) else xs[i] for i in range(0, len(xs), 2)]
    return xs[0]


def _topk_rounds(ss, labels, val_refs, idx_refs):
    big = jnp.iinfo(jnp.int32).max
    N = range(len(ss))
    tiles = [[s[t:t + 8, :] for t in range(0, s.shape[0], 8)] for s in ss]
    labs = [[lab[t:t + 8, :] for t in range(0, lab.shape[0], 8)] for lab in labels]
    for r in range(PEER_TOPK):
        ms = [jnp.max(_tree_reduce(jnp.maximum, tiles[n]), axis=0, keepdims=True) for n in N]
        firsts = []
        for n in N:
            acc = jnp.full(tiles[n][0].shape, big, I32)
            for tl, lb in zip(reversed(tiles[n]), reversed(labs[n])):
                acc = jnp.where(tl == ms[n], lb, acc)
            firsts.append(acc)
        idxs = [jnp.min(firsts[n], axis=0, keepdims=True) for n in N]
        for n in N:
            val_refs[n][r:r + 1, :] = ms[n]
            idx_refs[n][r:r + 1, :] = idxs[n]
        tiles = [[jnp.where(lb == idxs[n], -jnp.inf, tl) for tl, lb in zip(tiles[n], labs[n])] for n in N]


def _staircase_candidates(sv1, sv2):
    w = sv1.shape[1]
    iota8 = lax.broadcasted_iota(I32, (8, w), 0)
    vals, labs = [], []
    for a in range(8):
        nb = PEER_TOPK // (a + 1)
        for b0 in range(0, nb, 8):
            v = sv1[a:a + 1, :] + sv2[b0:b0 + 8, :]
            if nb - b0 < 8:
                v = jnp.where(iota8 < nb - b0, v, -jnp.inf)
            vals.append(v)
            labs.append(iota8 + (a * PEER_TOPK + b0))
    vals.append(sv1[8:16, :] + sv2[0:1, :])
    labs.append((iota8 + 8) * PEER_TOPK)
    return jnp.concatenate(vals, axis=0), jnp.concatenate(labs, axis=0)


def _pick_rows(sel, table):
    out = jnp.zeros(sel.shape, table.dtype)
    for a in range(PEER_TOPK):
        out = jnp.where(sel == a, table[a:a + 1, :], out)
    return out


def _peer_topk_kernel(h_ref, wq_ref, keys_ref, i_ref, j_ref, g_ref,
                      qs_ref, sv_ref, si_ref, tv_ref, tc_ref, *, tq):
    hd = pl.program_id(1)
    nq = 2 * PEER_HEADS

    @pl.when(hd == 0)
    def _():
        q = _dot(h_ref[...].astype(BF16), wq_ref[...])
        for c in range(nq):
            qs_ref[c] = q[:, c * PEER_HALF:(c + 1) * PEER_HALF].astype(BF16)

    units = [(hh, sl) for hh in range(TOPK_HEADS) for sl in range(tq // TOPK_W)]
    U = range(len(units))
    lanes = [slice(sl * TOPK_W, (sl + 1) * TOPK_W) for _, sl in units]
    rows = [slice(hh * PEER_TOPK, (hh + 1) * PEER_TOPK) for hh, _ in units]
    key_iota = lax.broadcasted_iota(I32, (PEER_NKEYS, TOPK_W), 0)
    probs = [(u, p) for u in U for p in range(2)]
    scores = [_dot_nt(keys_ref[p], qs_ref[2 * (TOPK_HEADS * hd + units[u][0]) + p, lanes[u], :])
              for u, p in probs]
    _topk_rounds(scores, [key_iota] * len(probs),
                 [sv_ref.at[u, p] for u, p in probs], [si_ref.at[u, p] for u, p in probs])
    cands = [_staircase_candidates(sv_ref[u, 0], sv_ref[u, 1]) for u in U]
    _topk_rounds([c[0] for c in cands], [c[1] for c in cands],
                 [tv_ref.at[u] for u in U], [tc_ref.at[u] for u in U])
    shift = int(math.log2(PEER_TOPK))
    for u in U:
        top_s = tv_ref[u]
        top_c = tc_ref[u]
        e = jnp.exp(top_s - top_s[0:1, :])
        g_ref[rows[u], lanes[u]] = e / jnp.sum(e, axis=0, keepdims=True)
        i_ref[rows[u], lanes[u]] = _pick_rows(top_c >> shift, si_ref[u, 0])
        j_ref[rows[u], lanes[u]] = _pick_rows(top_c & (PEER_TOPK - 1), si_ref[u, 1])


def _peer_topk(h2, wq_b, keys_b, tq):
    M, D = h2.shape
    nq = 2 * PEER_HEADS
    out_spec = pl.BlockSpec((TOPK_HEADS * PEER_TOPK, tq), lambda i, hd: (hd, i))
    nu = TOPK_HEADS * (tq // TOPK_W)
    np_ = PEER_HEADS * PEER_TOPK
    return pl.pallas_call(
        functools.partial(_peer_topk_kernel, tq=tq),
        grid=(M // tq, PEER_HEADS // TOPK_HEADS),
        in_specs=[pl.BlockSpec((tq, D), lambda i, hd: (i, 0)),
                  pl.BlockSpec(wq_b.shape, lambda i, hd: (0, 0)),
                  pl.BlockSpec(keys_b.shape, lambda i, hd: (0, 0, 0))],
        out_specs=[out_spec, out_spec, out_spec],
        out_shape=[jax.ShapeDtypeStruct((np_, M), I32), jax.ShapeDtypeStruct((np_, M), I32),
                   jax.ShapeDtypeStruct((np_, M), F32)],
        scratch_shapes=[pltpu.VMEM((nq, tq, PEER_HALF), BF16),
                        pltpu.VMEM((nu, 2, PEER_TOPK, TOPK_W), F32),
                        pltpu.VMEM((nu, 2, PEER_TOPK, TOPK_W), I32),
                        pltpu.VMEM((nu, PEER_TOPK, TOPK_W), F32),
                        pltpu.VMEM((nu, PEER_TOPK, TOPK_W), I32)],
        compiler_params=_cparams(("parallel", "arbitrary")),
        name="peer_topk",
    )(h2, wq_b, keys_b)


def _gate_kernel(i_ref, j_ref, g_ref, o_ref, *, tg):
    sub = lax.broadcasted_iota(I32, (PEER_NKEYS, PEER_HEADS * PEER_TOPK), 0)
    rows = o_ref.shape[2]

    def body(t, carry):
        irow = i_ref[pl.ds(t, 1), :]
        jrow = j_ref[pl.ds(t, 1), :]
        grow = g_ref[pl.ds(t, 1), :]
        at = jnp.where(sub == irow, grow, 0.0).astype(BF16)
        bt = jnp.where(sub == jrow, 1.0, 0.0).astype(BF16)
        gt = _dot_nt(at, bt)
        for blk in range(o_ref.shape[0]):
            o_ref[blk, t] = gt[blk * rows:(blk + 1) * rows, :]
        return carry

    lax.fori_loop(0, tg, body, 0, unroll=True)


def _gate_matrix(i_n, j_n, g_n, tg, tt, rows):
    M, P = i_n.shape
    nblk = PEER_NKEYS // rows
    per_tile = tt // tg
    spec = pl.BlockSpec((tg, P), lambda i: (i, 0))
    return pl.pallas_call(
        functools.partial(_gate_kernel, tg=tg),
        grid=(M // tg,),
        in_specs=[spec, spec, spec],
        out_specs=pl.BlockSpec((None, nblk, tg, rows, PEER_NKEYS),
                               lambda i: (i // per_tile, 0, i % per_tile, 0, 0)),
        out_shape=jax.ShapeDtypeStruct((M // tt, nblk, tt, rows, PEER_NKEYS), F32),
        compiler_params=_cparams(("parallel",)),
        name="peer_gates",
    )(i_n, j_n, g_n)


def _peer_dense_kernel(h_ref, u_ref, v_ref, gm_ref, g_ref, b_ref, o_ref, hb_ref, p_ref):
    j = pl.program_id(1)

    @pl.when(j == 0)
    def _():
        hb_ref[...] = h_ref[...].astype(BF16)
        o_ref[...] = jnp.zeros_like(o_ref)

    s = _dot_nt(hb_ref[...], u_ref[...])
    act = 0.5 * s * (1.0 + lax.erf(s * (2.0 ** -0.5)))
    tt = s.shape[0]
    rows = gm_ref.shape[0] // tt
    for r in range(rows):
        sl = slice(r * PEER_NKEYS, (r + 1) * PEER_NKEYS)
        p_ref[:, sl] = (act[:, sl] * gm_ref[pl.ds(r, tt, stride=rows), :]).astype(BF16)
    o_ref[...] += _dot(p_ref[...], v_ref[...])

    @pl.when(j == pl.num_programs(1) - 1)
    def _():
        y = DEEPNORM_ALPHA * h_ref[...] + o_ref[...]
        o_ref[...] = _layer_norm(y, g_ref[...], b_ref[...])


def _peer_dense(h2, u_b, v_b, gmat, ln_g, ln_b, tt, eb):
    M, D = h2.shape
    E = u_b.shape[0]
    rows = eb // PEER_NKEYS
    return pl.pallas_call(
        _peer_dense_kernel,
        grid=(M // tt, E // eb),
        in_specs=[pl.BlockSpec((tt, D), lambda i, j: (i, 0)),
                  pl.BlockSpec((eb, D), lambda i, j: (j, 0)),
                  pl.BlockSpec((eb, D), lambda i, j: (j, 0)),
                  pl.BlockSpec((None, None, tt * rows, PEER_NKEYS), lambda i, j: (i, j, 0, 0)),
                  pl.BlockSpec((1, D), lambda i, j: (0, 0)),
                  pl.BlockSpec((1, D), lambda i, j: (0, 0))],
        out_specs=pl.BlockSpec((tt, D), lambda i, j: (i, 0)),
        out_shape=jax.ShapeDtypeStruct((M, D), F32),
        scratch_shapes=[pltpu.VMEM((tt, D), BF16), pltpu.VMEM((tt, eb), BF16)],
        compiler_params=_cparams(("parallel", "arbitrary")),
        name="peer_dense",
    )(h2, u_b, v_b, gmat, ln_g.reshape(1, -1), ln_b.reshape(1, -1))


def _tile(n, pref):
    t = min(n, pref)
    assert n % t == 0
    return t


def _layer(x, w_in, conv_w, a_log, dt_bias, dn_norm_w, attn_sink, rel_bias, w_out,
           ln1_g, ln1_b, peer_wq, peer_keys, peer_u, peer_v, ln2_g, ln2_b):
    B, S, D = x.shape
    T = B * S
    x2 = x.reshape(T, D)

    proj, bg, bgT = _inproj(x2, w_in[:, :COL_TAIL].astype(BF16), w_in[:, COL_TAIL:], a_log, dt_bias,
                            _tile(T, 1024), COL_TAIL // 2)
    proj3 = proj.reshape(B, S, -1)

    sink_b = jnp.broadcast_to(attn_sink.astype(F32).reshape(ATTN_KV_HEADS, ATTN_GROUP, 1),
                              (ATTN_KV_HEADS, ATTN_GROUP, 128))
    attn = _attention(proj3, _attn_bias_table(rel_bias), sink_b)

    qkv = _conv(proj3, conv_w, _tile(S, 512))
    o_f, o_b = _deltanet(qkv, bg.reshape(B, S, 128), bgT)

    h = _outproj(attn.reshape(T, -1), o_f.reshape(T, -1), o_b.reshape(T, -1), proj, x2,
                 w_out.astype(BF16), dn_norm_w, ln1_g, ln1_b, _tile(T, 512))

    i_t, j_t, g_t = _peer_topk(h, peer_wq.astype(BF16), peer_keys.astype(BF16), _tile(T, 256))
    tt, eb = _tile(T, 1024), 512
    rows = eb // PEER_NKEYS
    gmat = _gate_matrix(i_t.T, j_t.T, g_t.T, _tile(T, 64), tt, rows)
    gmat = gmat.reshape(T // tt, PEER_NKEYS // rows, tt * rows, PEER_NKEYS)
    y = _peer_dense(h, peer_u.astype(BF16), peer_v.astype(BF16), gmat, ln2_g, ln2_b, tt, eb)
    return y.reshape(B, S, D)


def kernel(x, w_in, conv_w, a_log, dt_bias, dn_norm_w, attn_sink, rel_bias, w_out, ln1_g, ln1_b,
           peer_wq, peer_keys, peer_u, peer_v, ln2_g, ln2_b):
    for l in range(DEPTH):
        x = _layer(x, w_in[l], conv_w[l], a_log[l], dt_bias[l], dn_norm_w[l], attn_sink[l], rel_bias,
                   w_out[l], ln1_g[l], ln1_b[l], peer_wq[l], peer_keys[l], peer_u[l], peer_v[l],
                   ln2_g[l], ln2_b[l])
    return x
```

```python
import functools
import math

import jax
import jax.numpy as jnp
import numpy as np
from jax import lax
from jax.experimental import pallas as pl
from jax.experimental.pallas import tpu as pltpu

F32 = jnp.float32
BF16 = jnp.bfloat16
I32 = jnp.int32

HEAD_DIM = 128
ATTN_HEADS = 8
ATTN_KV_HEADS = 2
ATTN_GROUP = ATTN_HEADS // ATTN_KV_HEADS
WINDOW = 128
ATTN_BLOCK = 128
N_BUCKETS = 32
MAX_DISTANCE = 128
DN_HEADS = 8
DN_DIM = 128
CONV_WIDTH = 5
DN_CHUNK = 128
PEER_HEADS = 8
PEER_NKEYS = 128
PEER_HALF = 128
PEER_TOPK = 16
DEPTH = 1
DEEPNORM_ALPHA = (2.0 * DEPTH) ** 0.25
LN_EPS = 1e-5
RMS_EPS = 1e-6
NEG_BIG = -1e30

COL_AQ = 0
COL_AK = COL_AQ + ATTN_HEADS * HEAD_DIM
COL_AV = COL_AK + ATTN_KV_HEADS * HEAD_DIM
COL_DQ = COL_AV + ATTN_KV_HEADS * HEAD_DIM
COL_DZ = COL_DQ + 3 * DN_HEADS * DN_DIM
COL_TAIL = COL_DZ + DN_HEADS * DN_DIM
N_TAIL = 4 * DN_HEADS

VMEM_LIMIT = 62 * 1024 * 1024


def _cparams(sem):
    return pltpu.CompilerParams(dimension_semantics=sem, vmem_limit_bytes=VMEM_LIMIT)


def _dot(a, b):
    return jnp.dot(a, b, preferred_element_type=F32)


def _dot_nt(a, b, precision=None):
    return lax.dot_general(a, b, (((1,), (1,)), ((), ())), preferred_element_type=F32, precision=precision)


def _inproj_kernel(x_ref, w_ref, wt_ref, wtT_ref, alog_r_ref, dtb_r_ref, alog_c_ref, dtb_c_ref,
                   o_ref, bg_ref, bgT_ref, xb_ref):
    @pl.when(pl.program_id(1) == 0)
    def _():
        xb = x_ref[...].astype(BF16)
        xb_ref[...] = xb
        t = _dot(xb, wt_ref[...])
        tT = _dot_nt(wtT_ref[...], xb)
        lane = lax.broadcasted_iota(I32, t.shape, 1)
        g = -jnp.exp(alog_r_ref[...]) * jax.nn.softplus(t + dtb_r_ref[...])
        bg_ref[...] = jnp.where(lane < 2 * DN_HEADS, jax.nn.sigmoid(t), g)
        sub = lax.broadcasted_iota(I32, tT.shape, 0)
        gT = -jnp.exp(alog_c_ref[...]) * jax.nn.softplus(tT + dtb_c_ref[...])
        bgT_ref[...] = jnp.where(sub < 2 * DN_HEADS, jax.nn.sigmoid(tT), gT)

    o_ref[...] = _dot(xb_ref[...], w_ref[...]).astype(o_ref.dtype)


def _inproj(x2, w_main, w_tail, a_log, dt_bias, tm, tn):
    M, K = x2.shape
    N = w_main.shape[1]
    wt = jnp.zeros((K, 128), BF16).at[:, :N_TAIL].set(w_tail.astype(BF16))
    wtT = w_tail.astype(BF16).T
    al = jnp.zeros((N_TAIL,), F32).at[2 * DN_HEADS:].set(a_log.reshape(-1))
    db = jnp.zeros((N_TAIL,), F32).at[2 * DN_HEADS:].set(dt_bias.reshape(-1))
    al_r = jnp.zeros((1, 128), F32).at[0, :N_TAIL].set(al)
    db_r = jnp.zeros((1, 128), F32).at[0, :N_TAIL].set(db)
    full = lambda shape: pl.BlockSpec(shape, lambda i, j: (0, 0))
    return pl.pallas_call(
        _inproj_kernel,
        grid=(M // tm, N // tn),
        in_specs=[pl.BlockSpec((tm, K), lambda i, j: (i, 0)),
                  pl.BlockSpec((K, tn), lambda i, j: (0, j)),
                  full((K, 128)), full((N_TAIL, K)),
                  full((1, 128)), full((1, 128)), full((N_TAIL, 1)), full((N_TAIL, 1))],
        out_specs=[pl.BlockSpec((tm, tn), lambda i, j: (i, j)),
                   pl.BlockSpec((tm, 128), lambda i, j: (i, 0)),
                   pl.BlockSpec((N_TAIL, tm), lambda i, j: (0, i))],
        out_shape=[jax.ShapeDtypeStruct((M, N), BF16), jax.ShapeDtypeStruct((M, 128), F32),
                   jax.ShapeDtypeStruct((N_TAIL, M), F32)],
        scratch_shapes=[pltpu.VMEM((tm, K), BF16)],
        compiler_params=_cparams(("parallel", "arbitrary")),
        name="inproj",
    )(x2, w_main, wt, wtT, al_r, db_r, al.reshape(N_TAIL, 1), db.reshape(N_TAIL, 1))


def _t5_bucket(rel):
    nb = N_BUCKETS // 2
    max_exact = nb // 2
    ret = jnp.where(rel > 0, nb, 0)
    n = jnp.abs(rel)
    large = max_exact + (jnp.log(jnp.maximum(n, 1).astype(F32) / max_exact)
                         / math.log(MAX_DISTANCE / max_exact) * (nb - max_exact)).astype(I32)
    large = jnp.minimum(large, nb - 1)
    return ret + jnp.where(n < max_exact, n, large)


def _attn_bias_table(rel_bias):
    kj = jnp.arange(3 * ATTN_BLOCK)[:, None]
    qi = jnp.arange(ATTN_BLOCK)[None, :]
    rel = jnp.stack([kj - qi - variant * ATTN_BLOCK for variant in range(3)])
    bucket = jnp.where(jnp.abs(rel) <= WINDOW, _t5_bucket(rel), -1)[:, None]
    tab = jnp.full((3, ATTN_HEADS, 3 * ATTN_BLOCK, ATTN_BLOCK), NEG_BIG, F32)
    rb = rel_bias.astype(F32)
    for bk in range(N_BUCKETS):
        tab = jnp.where(bucket == bk, rb[bk][None, :, None, None], tab)
    return tab


def _attn_kernel(q_ref, k_ref, v_ref, bias_ref, sink_ref, o_ref, *, seq):
    n = pl.program_id(1)
    start = jnp.clip((n - 1) * ATTN_BLOCK, 0, seq - 3 * ATTN_BLOCK)
    start = pl.multiple_of(start, ATTN_BLOCK)
    H = range(ATTN_HEADS)
    hs = [slice(h * HEAD_DIM, (h + 1) * HEAD_DIM) for h in H]
    kvs = [slice(g * HEAD_DIM, (g + 1) * HEAD_DIM) for g in range(ATTN_KV_HEADS)]
    kb = [k_ref[pl.ds(start, 3 * ATTN_BLOCK), sl] for sl in kvs]
    vb = [v_ref[pl.ds(start, 3 * ATTN_BLOCK), sl] for sl in kvs]
    s = [_dot_nt(kb[h // ATTN_GROUP], q_ref[:, hs[h]]) * (HEAD_DIM ** -0.5) + bias_ref[h] for h in H]
    sk = [sink_ref[h // ATTN_GROUP, h % ATTN_GROUP:h % ATTN_GROUP + 1, 0:1] for h in H]
    m = [jnp.maximum(jnp.max(s[h], axis=0, keepdims=True), sk[h]) for h in H]
    p = [jnp.exp(s[h] - m[h]) for h in H]
    den = [jnp.sum(p[h], axis=0, keepdims=True) + jnp.exp(sk[h] - m[h]) for h in H]
    o = [lax.dot_general((p[h] / den[h]).astype(BF16), vb[h // ATTN_GROUP], (((0,), (0,)), ((), ())),
                         preferred_element_type=F32) for h in H]
    for h in H:
        o_ref[:, hs[h]] = o[h].astype(o_ref.dtype)


def _attention(proj3, bias_tab, sink_b):
    B, S, _ = proj3.shape
    nb = S // ATTN_BLOCK
    assert nb >= 3
    qw = ATTN_HEADS * HEAD_DIM
    kvw = ATTN_KV_HEADS * HEAD_DIM
    variant = lambda n: jnp.where(n == 0, 0, jnp.where(n == nb - 1, 2, 1))
    return pl.pallas_call(
        functools.partial(_attn_kernel, seq=S),
        grid=(B, nb),
        in_specs=[pl.BlockSpec((None, ATTN_BLOCK, qw), lambda b, n: (b, n, COL_AQ // qw)),
                  pl.BlockSpec((None, S, kvw), lambda b, n: (b, 0, COL_AK // kvw)),
                  pl.BlockSpec((None, S, kvw), lambda b, n: (b, 0, COL_AV // kvw)),
                  pl.BlockSpec((None, ATTN_HEADS, 3 * ATTN_BLOCK, ATTN_BLOCK),
                               lambda b, n: (variant(n), 0, 0, 0)),
                  pl.BlockSpec(sink_b.shape, lambda b, n: (0, 0, 0))],
        out_specs=pl.BlockSpec((None, ATTN_BLOCK, qw), lambda b, n: (b, n, 0)),
        out_shape=jax.ShapeDtypeStruct((B, S, qw), BF16),
        compiler_params=_cparams(("parallel", "arbitrary")),
        name="window_attn",
    )(proj3, proj3, proj3, bias_tab, sink_b)


CONV_COLS = 4 * DN_DIM
CONV_HALO = 16


def _conv_kernel(x_ref, prev_ref, next_ref, w_ref, o_ref, buf_ref, *, ts, n_sblk):
    s = pl.program_id(1)
    c = pl.program_id(2)
    pad = CONV_WIDTH // 2
    prev = jnp.where(s > 0, prev_ref[...].astype(F32), 0.0)
    nxt = jnp.where(s < n_sblk - 1, next_ref[...].astype(F32), 0.0)
    buf_ref[0:CONV_HALO, :] = prev
    buf_ref[CONV_HALO:CONV_HALO + ts, :] = x_ref[...].astype(F32)
    buf_ref[CONV_HALO + ts:CONV_HALO + ts + CONV_HALO, :] = nxt
    y = jnp.zeros((ts, CONV_COLS), F32)
    for w in range(CONV_WIDTH):
        y = y + buf_ref[CONV_HALO + w - pad:CONV_HALO + w - pad + ts, :] * w_ref[w:w + 1, :]
    y = jax.nn.silu(y)
    is_qk = c < 4
    qscale = jnp.where(c < 2, DN_DIM ** -0.5, 1.0).astype(F32)
    for hh in range(CONV_COLS // DN_DIM):
        yh = y[:, hh * DN_DIM:(hh + 1) * DN_DIM]
        nrm = yh * lax.rsqrt(jnp.sum(yh * yh, axis=-1, keepdims=True) + RMS_EPS) * qscale
        o_ref[:, hh * DN_DIM:(hh + 1) * DN_DIM] = jnp.where(is_qk, nrm, yh)


def _conv(proj3, conv_w, ts):
    B, S, _ = proj3.shape
    n_sblk = S // ts
    c0 = COL_DQ // CONV_COLS
    ncb = 3 * DN_HEADS * DN_DIM // CONV_COLS
    hb = ts // CONV_HALO
    nh = S // CONV_HALO
    return pl.pallas_call(
        functools.partial(_conv_kernel, ts=ts, n_sblk=n_sblk),
        grid=(B, n_sblk, ncb),
        in_specs=[pl.BlockSpec((None, ts, CONV_COLS), lambda b, s, c: (b, s, c0 + c)),
                  pl.BlockSpec((None, CONV_HALO, CONV_COLS),
                               lambda b, s, c: (b, jnp.maximum(s * hb - 1, 0), c0 + c)),
                  pl.BlockSpec((None, CONV_HALO, CONV_COLS),
                               lambda b, s, c: (b, jnp.minimum((s + 1) * hb, nh - 1), c0 + c)),
                  pl.BlockSpec((CONV_WIDTH, CONV_COLS), lambda b, s, c: (0, c))],
        out_specs=pl.BlockSpec((None, ts, CONV_COLS), lambda b, s, c: (b, s, c)),
        out_shape=jax.ShapeDtypeStruct((B, S, 3 * DN_HEADS * DN_DIM), F32),
        scratch_shapes=[pltpu.VMEM((ts + 2 * CONV_HALO, CONV_COLS), F32)],
        compiler_params=_cparams(("parallel", "parallel", "arbitrary")),
        name="short_conv",
    )(proj3, proj3, proj3, conv_w)


def _dn_kernel(qf, kf, vf, bgf, bgTf, qb, kb, vb, bgb, bgTb, of_ref, ob_ref, state_ref):
    @pl.when(pl.program_id(1) == 0)
    def _():
        state_ref[...] = jnp.zeros_like(state_ref)

    C = DN_CHUNK
    hi = lax.Precision.HIGHEST
    row = lax.broadcasted_iota(I32, (C, C), 0)
    col = lax.broadcasted_iota(I32, (C, C), 1)
    eye = jnp.where(row == col, 1.0, 0.0)
    refs = ((qf, kf, vf, bgf, bgTf, of_ref), (qb, kb, vb, bgb, bgTb, ob_ref))
    incl, strict, last, sels, bg, g_cum, g_cum_t = [], [], [], [], [], [], []
    for d in range(2):
        if d == 0:
            inc, stc, lst, lo, up = row >= col, row > col, C - 1, col, row
        else:
            inc, stc, lst, lo, up = row <= col, row < col, 0, row, col
        incl.append(inc)
        strict.append(stc)
        last.append(lst)
        sd, half = [], 1
        while half < C:
            sd.append(((row ^ col) < 2 * half) & ((up & half) != 0) & ((lo & half) == 0))
            half *= 2
        sels.append(sd)
        tri = inc.astype(F32)
        bg.append(refs[d][3][...])
        g_cum.append(jnp.dot(tri, bg[d], preferred_element_type=F32, precision=hi))
        g_cum_t.append(_dot_nt(refs[d][4][...], tri, precision=hi))

    P = [(d, h) for d in range(2) for h in range(DN_HEADS)]
    R = range(len(P))
    hs = [slice(h * DN_DIM, (h + 1) * DN_DIM) for _, h in P]
    q_ref = [refs[d][0] for d, _ in P]
    k_ref = [refs[d][1] for d, _ in P]
    v_ref = [refs[d][2] for d, _ in P]
    o_ref = [refs[d][5] for d, _ in P]
    cb = [d * DN_HEADS + h for d, h in P]
    cg = [2 * DN_HEADS + c for c in cb]
    beta_c = [bg[d][:, cb[i]:cb[i] + 1] for i, (d, _) in enumerate(P)]
    g_c = [g_cum[d][:, cg[i]:cg[i] + 1] for i, (d, _) in enumerate(P)]
    g_r = [g_cum_t[d][cg[i]:cg[i] + 1, :] for i, (d, _) in enumerate(P)]
    decay = [jnp.where(incl[P[i][0]], jnp.exp(jnp.where(incl[P[i][0]], g_c[i] - g_r[i], 0.0)), 0.0) for i in R]
    kbeta = [k_ref[i][:, hs[i]] * beta_c[i] for i in R]
    lmat = [jnp.where(strict[P[i][0]],
                      _dot_nt(kbeta[i].astype(BF16), k_ref[i][:, hs[i]].astype(BF16)) * decay[i], 0.0) for i in R]
    x = [eye - jnp.where(sels[P[i][0]][0], lmat[i], 0.0) for i in R]
    for lvl in range(1, len(sels[0])):
        xb = [x[i].astype(BF16) for i in R]
        t = [_dot(xb[i], jnp.where(sels[P[i][0]][lvl], lmat[i], 0.0).astype(BF16)) for i in R]
        x = [x[i] - _dot(t[i].astype(BF16), xb[i]) for i in R]
    e_g = [jnp.exp(g_c[i]) for i in R]
    uw = [_dot(x[i].astype(BF16),
               jnp.concatenate([v_ref[i][:, hs[i]] * beta_c[i], kbeta[i] * e_g[i]], axis=1).astype(BF16)) for i in R]
    amat = [(_dot_nt(q_ref[i][:, hs[i]].astype(BF16), k_ref[i][:, hs[i]].astype(BF16)) * decay[i]).astype(BF16)
            for i in R]
    g_last = [g_c[i][last[P[i][0]]:last[P[i][0]] + 1, :] for i in R]
    state_b = [state_ref[d, h].astype(BF16) for d, h in P]
    v_new = [(uw[i][:, :DN_DIM] - _dot(uw[i][:, DN_DIM:].astype(BF16), state_b[i])).astype(BF16) for i in R]
    for i in R:
        qg = (q_ref[i][:, hs[i]] * e_g[i]).astype(BF16)
        o_ref[i][:, hs[i]] = _dot(qg, state_b[i]) + _dot(amat[i], v_new[i])
    for i, (d, h) in enumerate(P):
        kd = k_ref[i][:, hs[i]] * jnp.exp(g_last[i] - g_c[i])
        state_ref[d, h] = state_ref[d, h] * jnp.exp(g_last[i]) + _dot(kd.T.astype(BF16), v_new[i])


def _deltanet(qkv, bg3, bgT):
    B, S, _ = qkv.shape
    C = DN_CHUNK
    N = S // C
    hw = DN_HEADS * DN_DIM

    def specs(nmap):
        return [pl.BlockSpec((None, C, hw), lambda b, n: (b, nmap(n), 0)),
                pl.BlockSpec((None, C, hw), lambda b, n: (b, nmap(n), 1)),
                pl.BlockSpec((None, C, hw), lambda b, n: (b, nmap(n), 2)),
                pl.BlockSpec((None, C, 128), lambda b, n: (b, nmap(n), 0)),
                pl.BlockSpec((N_TAIL, C), lambda b, n: (0, b * N + nmap(n)))]

    fwd = lambda n: n
    bwd = lambda n: N - 1 - n
    out_sd = jax.ShapeDtypeStruct((B, S, hw), F32)
    return pl.pallas_call(
        _dn_kernel,
        grid=(B, N),
        in_specs=specs(fwd) + specs(bwd),
        out_specs=[pl.BlockSpec((None, C, hw), lambda b, n: (b, fwd(n), 0)),
                   pl.BlockSpec((None, C, hw), lambda b, n: (b, bwd(n), 0))],
        out_shape=[out_sd, out_sd],
        scratch_shapes=[pltpu.VMEM((2, DN_HEADS, DN_DIM, DN_DIM), F32)],
        compiler_params=_cparams(("parallel", "arbitrary")),
        name="deltanet_scan",
    )(qkv, qkv, qkv, bg3, bgT, qkv, qkv, qkv, bg3, bgT)


def _layer_norm(y, g, b):
    mu = jnp.mean(y, axis=-1, keepdims=True)
    var = jnp.mean(jnp.square(y - mu), axis=-1, keepdims=True)
    return (y - mu) * lax.rsqrt(var + LN_EPS) * g + b


def _outproj_kernel(attn_ref, of_ref, ob_ref, z0_ref, z1_ref, x_ref, w_ref, nw_ref, g_ref, b_ref, h_ref, mix_ref):
    aw = ATTN_HEADS * HEAD_DIM
    mix_ref[:, :aw] = attn_ref[...]
    nw = nw_ref[...]
    half = z0_ref.shape[1] // DN_DIM
    for hh in range(DN_HEADS):
        sl = slice(hh * DN_DIM, (hh + 1) * DN_DIM)
        o = of_ref[:, sl] + ob_ref[:, sl]
        z_ref = z0_ref if hh < half else z1_ref
        zc = (hh % half) * DN_DIM
        z = z_ref[:, zc:zc + DN_DIM].astype(F32)
        o = o * lax.rsqrt(jnp.mean(o * o, axis=-1, keepdims=True) + RMS_EPS) * nw
        mix_ref[:, aw + hh * DN_DIM:aw + (hh + 1) * DN_DIM] = (o * jax.nn.silu(z)).astype(BF16)
    y = DEEPNORM_ALPHA * x_ref[...] + _dot(mix_ref[...], w_ref[...])
    h_ref[...] = _layer_norm(y, g_ref[...], b_ref[...])


def _outproj(attn2, of2, ob2, proj2, x2, w_out_b, norm_w, ln_g, ln_b, tm):
    M, D = x2.shape
    dw = DN_HEADS * DN_DIM
    zb = dw // 2
    z0 = COL_DZ // zb
    row = lambda w: pl.BlockSpec((tm, w), lambda i: (i, 0))
    full = lambda shape: pl.BlockSpec(shape, lambda i: (0, 0))
    return pl.pallas_call(
        _outproj_kernel,
        grid=(M // tm,),
        in_specs=[row(ATTN_HEADS * HEAD_DIM), row(dw), row(dw),
                  pl.BlockSpec((tm, zb), lambda i: (i, z0)), pl.BlockSpec((tm, zb), lambda i: (i, z0 + 1)),
                  row(D), full(w_out_b.shape), full((1, DN_DIM)), full((1, D)), full((1, D))],
        out_specs=row(D),
        out_shape=jax.ShapeDtypeStruct((M, D), F32),
        scratch_shapes=[pltpu.VMEM((tm, ATTN_HEADS * HEAD_DIM + dw), BF16)],
        compiler_params=_cparams(("parallel",)),
        name="outproj_ln",
    )(attn2, of2, ob2, proj2, proj2, x2, w_out_b, norm_w.reshape(1, -1), ln_g.reshape(1, -1), ln_b.reshape(1, -1))


TOPK_W = 128
TOPK_HEADS = 4


def _tree_reduce(fn, xs):
    xs = list(xs)
    while len(xs) > 1:
        xs = [fn(xs[i], xs[i + 1]) if i + 1 < len(xs) else xs[i] for i in range(0, len(xs), 2)]
    return xs[0]


def _topk_rounds(ss, labels, val_refs, idx_refs):
    big = jnp.iinfo(jnp.int32).max
    N = range(len(ss))
    tiles = [[s[t:t + 8, :] for t in range(0, s.shape[0], 8)] for s in ss]
    labs = [[lab[t:t + 8, :] for t in range(0, lab.shape[0], 8)] for lab in labels]
    for r in range(PEER_TOPK):
        ms = [jnp.max(_tree_reduce(jnp.maximum, tiles[n]), axis=0, keepdims=True) for n in N]
        firsts = []
        for n in N:
            acc = jnp.full(tiles[n][0].shape, big, I32)
            for tl, lb in zip(reversed(tiles[n]), reversed(labs[n])):
                acc = jnp.where(tl == ms[n], lb, acc)
            firsts.append(acc)
        idxs = [jnp.min(firsts[n], axis=0, keepdims=True) for n in N]
        for n in N:
            val_refs[n][r:r + 1, :] = ms[n]
            idx_refs[n][r:r + 1, :] = idxs[n]
        tiles = [[jnp.where(lb == idxs[n], -jnp.inf, tl) for tl, lb in zip(tiles[n], labs[n])] for n in N]


def _staircase_candidates(sv1, sv2):
    w = sv1.shape[1]
    iota8 = lax.broadcasted_iota(I32, (8, w), 0)
    vals, labs = [], []
    for a in range(8):
        nb = PEER_TOPK // (a + 1)
        for b0 in range(0, nb, 8):
            v = sv1[a:a + 1, :] + sv2[b0:b0 + 8, :]
            if nb - b0 < 8:
                v = jnp.where(iota8 < nb - b0, v, -jnp.inf)
            vals.append(v)
            labs.append(iota8 + (a * PEER_TOPK + b0))
    vals.append(sv1[8:16, :] + sv2[0:1, :])
    labs.append((iota8 + 8) * PEER_TOPK)
    return jnp.concatenate(vals, axis=0), jnp.concatenate(labs, axis=0)


def _pick_rows(sel, table):
    out = jnp.zeros(sel.shape, table.dtype)
    for a in range(PEER_TOPK):
        out = jnp.where(sel == a, table[a:a + 1, :], out)
    return out


def _peer_topk_kernel(h_ref, wq_ref, keys_ref, i_ref, j_ref, g_ref,
                      qs_ref, sv_ref, si_ref, tv_ref, tc_ref, *, tq):
    hd = pl.program_id(1)
    nq = 2 * PEER_HEADS

    @pl.when(hd == 0)
    def _():
        q = _dot(h_ref[...].astype(BF16), wq_ref[...])
        for c in range(nq):
            qs_ref[c] = q[:, c * PEER_HALF:(c + 1) * PEER_HALF].astype(BF16)

    units = [(hh, sl) for hh in range(TOPK_HEADS) for sl in range(tq // TOPK_W)]
    U = range(len(units))
    lanes = [slice(sl * TOPK_W, (sl + 1) * TOPK_W) for _, sl in units]
    rows = [slice(hh * PEER_TOPK, (hh + 1) * PEER_TOPK) for hh, _ in units]
    key_iota = lax.broadcasted_iota(I32, (PEER_NKEYS, TOPK_W), 0)
    probs = [(u, p) for u in U for p in range(2)]
    scores = [_dot_nt(keys_ref[p], qs_ref[2 * (TOPK_HEADS * hd + units[u][0]) + p, lanes[u], :])
              for u, p in probs]
    _topk_rounds(scores, [key_iota] * len(probs),
                 [sv_ref.at[u, p] for u, p in probs], [si_ref.at[u, p] for u, p in probs])
    cands = [_staircase_candidates(sv_ref[u, 0], sv_ref[u, 1]) for u in U]
    _topk_rounds([c[0] for c in cands], [c[1] for c in cands],
                 [tv_ref.at[u] for u in U], [tc_ref.at[u] for u in U])
    shift = int(math.log2(PEER_TOPK))
    for u in U:
        top_s = tv_ref[u]
        top_c = tc_ref[u]
        e = jnp.exp(top_s - top_s[0:1, :])
        g_ref[rows[u], lanes[u]] = e / jnp.sum(e, axis=0, keepdims=True)
        i_ref[rows[u], lanes[u]] = _pick_rows(top_c >> shift, si_ref[u, 0])
        j_ref[rows[u], lanes[u]] = _pick_rows(top_c & (PEER_TOPK - 1), si_ref[u, 1])


def _peer_topk(h2, wq_b, keys_b, tq):
    M, D = h2.shape
    nq = 2 * PEER_HEADS
    out_spec = pl.BlockSpec((TOPK_HEADS * PEER_TOPK, tq), lambda i, hd: (hd, i))
    nu = TOPK_HEADS * (tq // TOPK_W)
    np_ = PEER_HEADS * PEER_TOPK
    return pl.pallas_call(
        functools.partial(_peer_topk_kernel, tq=tq),
        grid=(M // tq, PEER_HEADS // TOPK_HEADS),
        in_specs=[pl.BlockSpec((tq, D), lambda i, hd: (i, 0)),
                  pl.BlockSpec(wq_b.shape, lambda i, hd: (0, 0)),
                  pl.BlockSpec(keys_b.shape, lambda i, hd: (0, 0, 0))],
        out_specs=[out_spec, out_spec, out_spec],
        out_shape=[jax.ShapeDtypeStruct((np_, M), I32), jax.ShapeDtypeStruct((np_, M), I32),
                   jax.ShapeDtypeStruct((np_, M), F32)],
        scratch_shapes=[pltpu.VMEM((nq, tq, PEER_HALF), BF16),
                        pltpu.VMEM((nu, 2, PEER_TOPK, TOPK_W), F32),
                        pltpu.VMEM((nu, 2, PEER_TOPK, TOPK_W), I32),
                        pltpu.VMEM((nu, PEER_TOPK, TOPK_W), F32),
                        pltpu.VMEM((nu, PEER_TOPK, TOPK_W), I32)],
        compiler_params=_cparams(("parallel", "arbitrary")),
        name="peer_topk",
    )(h2, wq_b, keys_b)


def _gate_kernel(i_ref, j_ref, g_ref, o_ref, *, tg):
    sub = lax.broadcasted_iota(I32, (PEER_NKEYS, PEER_HEADS * PEER_TOPK), 0)
    rows = o_ref.shape[2]

    def body(t, carry):
        irow = i_ref[pl.ds(t, 1), :]
        jrow = j_ref[pl.ds(t, 1), :]
        grow = g_ref[pl.ds(t, 1), :]
        at = jnp.where(sub == irow, grow, 0.0).astype(BF16)
        bt = jnp.where(sub == jrow, 1.0, 0.0).astype(BF16)
        gt = _dot_nt(at, bt)
        for blk in range(o_ref.shape[0]):
            o_ref[blk, t] = gt[blk * rows:(blk + 1) * rows, :]
        return carry

    lax.fori_loop(0, tg, body, 0, unroll=True)


def _gate_matrix(i_n, j_n, g_n, tg, tt, rows):
    M, P = i_n.shape
    nblk = PEER_NKEYS // rows
    per_tile = tt // tg
    spec = pl.BlockSpec((tg, P), lambda i: (i, 0))
    return pl.pallas_call(
        functools.partial(_gate_kernel, tg=tg),
        grid=(M // tg,),
        in_specs=[spec, spec, spec],
        out_specs=pl.BlockSpec((None, nblk, tg, rows, PEER_NKEYS),
                               lambda i: (i // per_tile, 0, i % per_tile, 0, 0)),
        out_shape=jax.ShapeDtypeStruct((M // tt, nblk, tt, rows, PEER_NKEYS), F32),
        compiler_params=_cparams(("parallel",)),
        name="peer_gates",
    )(i_n, j_n, g_n)


def _peer_dense_kernel(h_ref, u_ref, v_ref, gm_ref, g_ref, b_ref, o_ref, hb_ref, p_ref):
    j = pl.program_id(1)

    @pl.when(j == 0)
    def _():
        hb_ref[...] = h_ref[...].astype(BF16)
        o_ref[...] = jnp.zeros_like(o_ref)

    s = _dot_nt(hb_ref[...], u_ref[...])
    act = 0.5 * s * (1.0 + lax.erf(s * (2.0 ** -0.5)))
    tt = s.shape[0]
    rows = gm_ref.shape[0] // tt
    for r in range(rows):
        sl = slice(r * PEER_NKEYS, (r + 1) * PEER_NKEYS)
        p_ref[:, sl] = (act[:, sl] * gm_ref[pl.ds(r, tt, stride=rows), :]).astype(BF16)
    o_ref[...] += _dot(p_ref[...], v_ref[...])

    @pl.when(j == pl.num_programs(1) - 1)
    def _():
        y = DEEPNORM_ALPHA * h_ref[...] + o_ref[...]
        o_ref[...] = _layer_norm(y, g_ref[...], b_ref[...])


def _peer_dense(h2, u_b, v_b, gmat, ln_g, ln_b, tt, eb):
    M, D = h2.shape
    E = u_b.shape[0]
    rows = eb // PEER_NKEYS
    return pl.pallas_call(
        _peer_dense_kernel,
        grid=(M // tt, E // eb),
        in_specs=[pl.BlockSpec((tt, D), lambda i, j: (i, 0)),
                  pl.BlockSpec((eb, D), lambda i, j: (j, 0)),
                  pl.BlockSpec((eb, D), lambda i, j: (j, 0)),
                  pl.BlockSpec((None, None, tt * rows, PEER_NKEYS), lambda i, j: (i, j, 0, 0)),
                  pl.BlockSpec((1, D), lambda i, j: (0, 0)),
                  pl.BlockSpec((1, D), lambda i, j: (0, 0))],
        out_specs=pl.BlockSpec((tt, D), lambda i, j: (i, 0)),
        out_shape=jax.ShapeDtypeStruct((M, D), F32),
        scratch_shapes=[pltpu.VMEM((tt, D), BF16), pltpu.VMEM((tt, eb), BF16)],
        compiler_params=_cparams(("parallel", "arbitrary")),
        name="peer_dense",
    )(h2, u_b, v_b, gmat, ln_g.reshape(1, -1), ln_b.reshape(1, -1))


def _tile(n, pref):
    t = min(n, pref)
    assert n % t == 0
    return t


def _layer(x, w_in, conv_w, a_log, dt_bias, dn_norm_w, attn_sink, rel_bias, w_out,
           ln1_g, ln1_b, peer_wq, peer_keys, peer_u, peer_v, ln2_g, ln2_b):
    B, S, D = x.shape
    T = B * S
    x2 = x.reshape(T, D)

    proj, bg, bgT = _inproj(x2, w_in[:, :COL_TAIL].astype(BF16), w_in[:, COL_TAIL:], a_log, dt_bias,
                            _tile(T, 1024), COL_TAIL // 2)
    proj3 = proj.reshape(B, S, -1)

    sink_b = jnp.broadcast_to(attn_sink.astype(F32).reshape(ATTN_KV_HEADS, ATTN_GROUP, 1),
                              (ATTN_KV_HEADS, ATTN_GROUP, 128))
    attn = _attention(proj3, _attn_bias_table(rel_bias), sink_b)

    qkv = _conv(proj3, conv_w, _tile(S, 1024))
    o_f, o_b = _deltanet(qkv, bg.reshape(B, S, 128), bgT)

    h = _outproj(attn.reshape(T, -1), o_f.reshape(T, -1), o_b.reshape(T, -1), proj, x2,
                 w_out.astype(BF16), dn_norm_w, ln1_g, ln1_b, _tile(T, 512))

    i_t, j_t, g_t = _peer_topk(h, peer_wq.astype(BF16), peer_keys.astype(BF16), _tile(T, 256))
    tt, eb = _tile(T, 1024), 512
    rows = eb // PEER_NKEYS
    gmat = _gate_matrix(i_t.T, j_t.T, g_t.T, _tile(T, 128), tt, rows)
    gmat = gmat.reshape(T // tt, PEER_NKEYS // rows, tt * rows, PEER_NKEYS)
    y = _peer_dense(h, peer_u.astype(BF16), peer_v.astype(BF16), gmat, ln2_g, ln2_b, tt, eb)
    return y.reshape(B, S, D)


def kernel(x, w_in, conv_w, a_log, dt_bias, dn_norm_w, attn_sink, rel_bias, w_out, ln1_g, ln1_b,
           peer_wq, peer_keys, peer_u, peer_v, ln2_g, ln2_b):
    for l in range(DEPTH):
        x = _layer(x, w_in[l], conv_w[l], a_log[l], dt_bias[l], dn_norm_w[l], attn_sink[l], rel_bias,
                   w_out[l], ln1_g[l], ln1_b[l], peer_wq[l], peer_keys[l], peer_u[l], peer_v[l],
                   ln2_g[l], ln2_b[l])
    return x
```

```python
import functools
import math

import jax
import jax.numpy as jnp
import numpy as np
from jax import lax
from jax.experimental import pallas as pl
from jax.experimental.pallas import tpu as pltpu

F32 = jnp.float32
BF16 = jnp.bfloat16
I32 = jnp.int32

HEAD_DIM = 128
ATTN_HEADS = 8
ATTN_KV_HEADS = 2
ATTN_GROUP = ATTN_HEADS // ATTN_KV_HEADS
WINDOW = 128
ATTN_BLOCK = 128
N_BUCKETS = 32
MAX_DISTANCE = 128
DN_HEADS = 8
DN_DIM = 128
CONV_WIDTH = 5
DN_CHUNK = 128
PEER_HEADS = 8
PEER_NKEYS = 128
PEER_HALF = 128
PEER_TOPK = 16
DEPTH = 1
DEEPNORM_ALPHA = (2.0 * DEPTH) ** 0.25
LN_EPS = 1e-5
RMS_EPS = 1e-6
NEG_BIG = -1e30

COL_AQ = 0
COL_AK = COL_AQ + ATTN_HEADS * HEAD_DIM
COL_AV = COL_AK + ATTN_KV_HEADS * HEAD_DIM
COL_DQ = COL_AV + ATTN_KV_HEADS * HEAD_DIM
COL_DZ = COL_DQ + 3 * DN_HEADS * DN_DIM
COL_TAIL = COL_DZ + DN_HEADS * DN_DIM
N_TAIL = 4 * DN_HEADS

VMEM_LIMIT = 62 * 1024 * 1024


def _cparams(sem):
    return pltpu.CompilerParams(dimension_semantics=sem, vmem_limit_bytes=VMEM_LIMIT)


def _dot(a, b):
    return jnp.dot(a, b, preferred_element_type=F32)


def _dot_nt(a, b, precision=None):
    return lax.dot_general(a, b, (((1,), (1,)), ((), ())), preferred_element_type=F32, precision=precision)


def _inproj_kernel(x_ref, w_ref, wt_ref, wtT_ref, alog_r_ref, dtb_r_ref, alog_c_ref, dtb_c_ref,
                   o_ref, bg_ref, bgT_ref, xb_ref):
    @pl.when(pl.program_id(1) == 0)
    def _():
        xb = x_ref[...].astype(BF16)
        xb_ref[...] = xb
        t = _dot(xb, wt_ref[...])
        tT = _dot_nt(wtT_ref[...], xb)
        lane = lax.broadcasted_iota(I32, t.shape, 1)
        g = -jnp.exp(alog_r_ref[...]) * jax.nn.softplus(t + dtb_r_ref[...])
        bg_ref[...] = jnp.where(lane < 2 * DN_HEADS, jax.nn.sigmoid(t), g)
        sub = lax.broadcasted_iota(I32, tT.shape, 0)
        gT = -jnp.exp(alog_c_ref[...]) * jax.nn.softplus(tT + dtb_c_ref[...])
        bgT_ref[...] = jnp.where(sub < 2 * DN_HEADS, jax.nn.sigmoid(tT), gT)

    o_ref[...] = _dot(xb_ref[...], w_ref[...]).astype(o_ref.dtype)


def _inproj(x2, w_main, w_tail, a_log, dt_bias, tm, tn):
    M, K = x2.shape
    N = w_main.shape[1]
    wt = jnp.zeros((K, 128), BF16).at[:, :N_TAIL].set(w_tail.astype(BF16))
    wtT = w_tail.astype(BF16).T
    al = jnp.zeros((N_TAIL,), F32).at[2 * DN_HEADS:].set(a_log.reshape(-1))
    db = jnp.zeros((N_TAIL,), F32).at[2 * DN_HEADS:].set(dt_bias.reshape(-1))
    al_r = jnp.zeros((1, 128), F32).at[0, :N_TAIL].set(al)
    db_r = jnp.zeros((1, 128), F32).at[0, :N_TAIL].set(db)
    full = lambda shape: pl.BlockSpec(shape, lambda i, j: (0, 0))
    return pl.pallas_call(
        _inproj_kernel,
        grid=(M // tm, N // tn),
        in_specs=[pl.BlockSpec((tm, K), lambda i, j: (i, 0)),
                  pl.BlockSpec((K, tn), lambda i, j: (0, j)),
                  full((K, 128)), full((N_TAIL, K)),
                  full((1, 128)), full((1, 128)), full((N_TAIL, 1)), full((N_TAIL, 1))],
        out_specs=[pl.BlockSpec((tm, tn), lambda i, j: (i, j)),
                   pl.BlockSpec((tm, 128), lambda i, j: (i, 0)),
                   pl.BlockSpec((N_TAIL, tm), lambda i, j: (0, i))],
        out_shape=[jax.ShapeDtypeStruct((M, N), BF16), jax.ShapeDtypeStruct((M, 128), F32),
                   jax.ShapeDtypeStruct((N_TAIL, M), F32)],
        scratch_shapes=[pltpu.VMEM((tm, K), BF16)],
        compiler_params=_cparams(("parallel", "arbitrary")),
        name="inproj",
    )(x2, w_main, wt, wtT, al_r, db_r, al.reshape(N_TAIL, 1), db.reshape(N_TAIL, 1))


def _t5_bucket(rel):
    nb = N_BUCKETS // 2
    max_exact = nb // 2
    ret = jnp.where(rel > 0, nb, 0)
    n = jnp.abs(rel)
    large = max_exact + (jnp.log(jnp.maximum(n, 1).astype(F32) / max_exact)
                         / math.log(MAX_DISTANCE / max_exact) * (nb - max_exact)).astype(I32)
    large = jnp.minimum(large, nb - 1)
    return ret + jnp.where(n < max_exact, n, large)


def _attn_bias_table(rel_bias):
    kj = jnp.arange(3 * ATTN_BLOCK)[:, None]
    qi = jnp.arange(ATTN_BLOCK)[None, :]
    rel = jnp.stack([kj - qi - variant * ATTN_BLOCK for variant in range(3)])
    bucket = jnp.where(jnp.abs(rel) <= WINDOW, _t5_bucket(rel), -1)[:, None]
    tab = jnp.full((3, ATTN_HEADS, 3 * ATTN_BLOCK, ATTN_BLOCK), NEG_BIG, F32)
    rb = rel_bias.astype(F32)
    for bk in range(N_BUCKETS):
        tab = jnp.where(bucket == bk, rb[bk][None, :, None, None], tab)
    return tab


def _attn_kernel(q_ref, k_ref, v_ref, bias_ref, sink_ref, o_ref, *, seq):
    n = pl.program_id(1)
    start = jnp.clip((n - 1) * ATTN_BLOCK, 0, seq - 3 * ATTN_BLOCK)
    start = pl.multiple_of(start, ATTN_BLOCK)
    H = range(ATTN_HEADS)
    hs = [slice(h * HEAD_DIM, (h + 1) * HEAD_DIM) for h in H]
    kvs = [slice(g * HEAD_DIM, (g + 1) * HEAD_DIM) for g in range(ATTN_KV_HEADS)]
    kb = [k_ref[pl.ds(start, 3 * ATTN_BLOCK), sl] for sl in kvs]
    vb = [v_ref[pl.ds(start, 3 * ATTN_BLOCK), sl] for sl in kvs]
    s = [_dot_nt(kb[h // ATTN_GROUP], q_ref[:, hs[h]]) * (HEAD_DIM ** -0.5) + bias_ref[h] for h in H]
    sk = [sink_ref[h // ATTN_GROUP, h % ATTN_GROUP:h % ATTN_GROUP + 1, 0:1] for h in H]
    m = [jnp.maximum(jnp.max(s[h], axis=0, keepdims=True), sk[h]) for h in H]
    p = [jnp.exp(s[h] - m[h]) for h in H]
    den = [jnp.sum(p[h], axis=0, keepdims=True) + jnp.exp(sk[h] - m[h]) for h in H]
    o = [lax.dot_general((p[h] / den[h]).astype(BF16), vb[h // ATTN_GROUP], (((0,), (0,)), ((), ())),
                         preferred_element_type=F32) for h in H]
    for h in H:
        o_ref[:, hs[h]] = o[h].astype(o_ref.dtype)


def _attention(proj3, bias_tab, sink_b):
    B, S, _ = proj3.shape
    nb = S // ATTN_BLOCK
    assert nb >= 3
    qw = ATTN_HEADS * HEAD_DIM
    kvw = ATTN_KV_HEADS * HEAD_DIM
    variant = lambda n: jnp.where(n == 0, 0, jnp.where(n == nb - 1, 2, 1))
    return pl.pallas_call(
        functools.partial(_attn_kernel, seq=S),
        grid=(B, nb),
        in_specs=[pl.BlockSpec((None, ATTN_BLOCK, qw), lambda b, n: (b, n, COL_AQ // qw)),
                  pl.BlockSpec((None, S, kvw), lambda b, n: (b, 0, COL_AK // kvw)),
                  pl.BlockSpec((None, S, kvw), lambda b, n: (b, 0, COL_AV // kvw)),
                  pl.BlockSpec((None, ATTN_HEADS, 3 * ATTN_BLOCK, ATTN_BLOCK),
                               lambda b, n: (variant(n), 0, 0, 0)),
                  pl.BlockSpec(sink_b.shape, lambda b, n: (0, 0, 0))],
        out_specs=pl.BlockSpec((None, ATTN_BLOCK, qw), lambda b, n: (b, n, 0)),
        out_shape=jax.ShapeDtypeStruct((B, S, qw), BF16),
        compiler_params=_cparams(("parallel", "arbitrary")),
        name="window_attn",
    )(proj3, proj3, proj3, bias_tab, sink_b)


CONV_COLS = 4 * DN_DIM
CONV_HALO = 16


def _conv_kernel(x_ref, prev_ref, next_ref, w_ref, o_ref, buf_ref, *, ts, n_sblk):
    s = pl.program_id(1)
    c = pl.program_id(2)
    pad = CONV_WIDTH // 2
    prev = jnp.where(s > 0, prev_ref[...].astype(F32), 0.0)
    nxt = jnp.where(s < n_sblk - 1, next_ref[...].astype(F32), 0.0)
    buf_ref[0:CONV_HALO, :] = prev
    buf_ref[CONV_HALO:CONV_HALO + ts, :] = x_ref[...].astype(F32)
    buf_ref[CONV_HALO + ts:CONV_HALO + ts + CONV_HALO, :] = nxt
    y = jnp.zeros((ts, CONV_COLS), F32)
    for w in range(CONV_WIDTH):
        y = y + buf_ref[CONV_HALO + w - pad:CONV_HALO + w - pad + ts, :] * w_ref[w:w + 1, :]
    y = jax.nn.silu(y)
    is_qk = c < 4
    qscale = jnp.where(c < 2, DN_DIM ** -0.5, 1.0).astype(F32)
    for hh in range(CONV_COLS // DN_DIM):
        yh = y[:, hh * DN_DIM:(hh + 1) * DN_DIM]
        nrm = yh * lax.rsqrt(jnp.sum(yh * yh, axis=-1, keepdims=True) + RMS_EPS) * qscale
        o_ref[:, hh * DN_DIM:(hh + 1) * DN_DIM] = jnp.where(is_qk, nrm, yh)


def _conv(proj3, conv_w, ts):
    B, S, _ = proj3.shape
    n_sblk = S // ts
    c0 = COL_DQ // CONV_COLS
    ncb = 3 * DN_HEADS * DN_DIM // CONV_COLS
    hb = ts // CONV_HALO
    nh = S // CONV_HALO
    return pl.pallas_call(
        functools.partial(_conv_kernel, ts=ts, n_sblk=n_sblk),
        grid=(B, n_sblk, ncb),
        in_specs=[pl.BlockSpec((None, ts, CONV_COLS), lambda b, s, c: (b, s, c0 + c)),
                  pl.BlockSpec((None, CONV_HALO, CONV_COLS),
                               lambda b, s, c: (b, jnp.maximum(s * hb - 1, 0), c0 + c)),
                  pl.BlockSpec((None, CONV_HALO, CONV_COLS),
                               lambda b, s, c: (b, jnp.minimum((s + 1) * hb, nh - 1), c0 + c)),
                  pl.BlockSpec((CONV_WIDTH, CONV_COLS), lambda b, s, c: (0, c))],
        out_specs=pl.BlockSpec((None, ts, CONV_COLS), lambda b, s, c: (b, s, c)),
        out_shape=jax.ShapeDtypeStruct((B, S, 3 * DN_HEADS * DN_DIM), F32),
        scratch_shapes=[pltpu.VMEM((ts + 2 * CONV_HALO, CONV_COLS), F32)],
        compiler_params=_cparams(("parallel", "parallel", "arbitrary")),
        name="short_conv",
    )(proj3, proj3, proj3, conv_w)


def _dn_kernel(qf, kf, vf, bgf, bgTf, qb, kb, vb, bgb, bgTb, of_ref, ob_ref, state_ref):
    @pl.when(pl.program_id(1) == 0)
    def _():
        state_ref[...] = jnp.zeros_like(state_ref)

    C = DN_CHUNK
    hi = lax.Precision.HIGHEST
    row = lax.broadcasted_iota(I32, (C, C), 0)
    col = lax.broadcasted_iota(I32, (C, C), 1)
    eye = jnp.where(row == col, 1.0, 0.0)
    refs = ((qf, kf, vf, bgf, bgTf, of_ref), (qb, kb, vb, bgb, bgTb, ob_ref))
    incl, strict, last, sels, bg, g_cum, g_cum_t = [], [], [], [], [], [], []
    for d in range(2):
        if d == 0:
            inc, stc, lst, lo, up = row >= col, row > col, C - 1, col, row
        else:
            inc, stc, lst, lo, up = row <= col, row < col, 0, row, col
        incl.append(inc)
        strict.append(stc)
        last.append(lst)
        sd, half = [], 1
        while half < C:
            sd.append(((row ^ col) < 2 * half) & ((up & half) != 0) & ((lo & half) == 0))
            half *= 2
        sels.append(sd)
        tri = inc.astype(F32)
        bg.append(refs[d][3][...])
        g_cum.append(jnp.dot(tri, bg[d], preferred_element_type=F32, precision=hi))
        g_cum_t.append(_dot_nt(refs[d][4][...], tri, precision=hi))

    P = [(d, h) for d in range(2) for h in range(DN_HEADS)]
    R = range(len(P))
    hs = [slice(h * DN_DIM, (h + 1) * DN_DIM) for _, h in P]
    q_ref = [refs[d][0] for d, _ in P]
    k_ref = [refs[d][1] for d, _ in P]
    v_ref = [refs[d][2] for d, _ in P]
    o_ref = [refs[d][5] for d, _ in P]
    cb = [d * DN_HEADS + h for d, h in P]
    cg = [2 * DN_HEADS + c for c in cb]
    beta_c = [bg[d][:, cb[i]:cb[i] + 1] for i, (d, _) in enumerate(P)]
    g_c = [g_cum[d][:, cg[i]:cg[i] + 1] for i, (d, _) in enumerate(P)]
    g_r = [g_cum_t[d][cg[i]:cg[i] + 1, :] for i, (d, _) in enumerate(P)]
    decay = [jnp.where(incl[P[i][0]], jnp.exp(jnp.where(incl[P[i][0]], g_c[i] - g_r[i], 0.0)), 0.0) for i in R]
    kbeta = [k_ref[i][:, hs[i]] * beta_c[i] for i in R]
    lmat = [jnp.where(strict[P[i][0]],
                      _dot_nt(kbeta[i].astype(BF16), k_ref[i][:, hs[i]].astype(BF16)) * decay[i], 0.0) for i in R]
    x = [eye - jnp.where(sels[P[i][0]][0], lmat[i], 0.0) for i in R]
    for lvl in range(1, len(sels[0])):
        xb = [x[i].astype(BF16) for i in R]
        t = [_dot(xb[i], jnp.where(sels[P[i][0]][lvl], lmat[i], 0.0).astype(BF16)) for i in R]
        x = [x[i] - _dot(t[i].astype(BF16), xb[i]) for i in R]
    e_g = [jnp.exp(g_c[i]) for i in R]
    uw = [_dot(x[i].astype(BF16),
               jnp.concatenate([v_ref[i][:, hs[i]] * beta_c[i], kbeta[i] * e_g[i]], axis=1).astype(BF16)) for i in R]
    amat = [(_dot_nt(q_ref[i][:, hs[i]].astype(BF16), k_ref[i][:, hs[i]].astype(BF16)) * decay[i]).astype(BF16)
            for i in R]
    g_last = [g_c[i][last[P[i][0]]:last[P[i][0]] + 1, :] for i in R]
    state_b = [state_ref[d, h].astype(BF16) for d, h in P]
    v_new = [(uw[i][:, :DN_DIM] - _dot(uw[i][:, DN_DIM:].astype(BF16), state_b[i])).astype(BF16) for i in R]
    for i in R:
        qg = (q_ref[i][:, hs[i]] * e_g[i]).astype(BF16)
        o_ref[i][:, hs[i]] = _dot(qg, state_b[i]) + _dot(amat[i], v_new[i])
    for i, (d, h) in enumerate(P):
        kd = k_ref[i][:, hs[i]] * jnp.exp(g_last[i] - g_c[i])
        state_ref[d, h] = state_ref[d, h] * jnp.exp(g_last[i]) + _dot(kd.T.astype(BF16), v_new[i])


def _deltanet(qkv, bg3, bgT):
    B, S, _ = qkv.shape
    C = DN_CHUNK
    N = S // C
    hw = DN_HEADS * DN_DIM

    def specs(nmap):
        return [pl.BlockSpec((None, C, hw), lambda b, n: (b, nmap(n), 0)),
                pl.BlockSpec((None, C, hw), lambda b, n: (b, nmap(n), 1)),
                pl.BlockSpec((None, C, hw), lambda b, n: (b, nmap(n), 2)),
                pl.BlockSpec((None, C, 128), lambda b, n: (b, nmap(n), 0)),
                pl.BlockSpec((N_TAIL, C), lambda b, n: (0, b * N + nmap(n)))]

    fwd = lambda n: n
    bwd = lambda n: N - 1 - n
    out_sd = jax.ShapeDtypeStruct((B, S, hw), F32)
    return pl.pallas_call(
        _dn_kernel,
        grid=(B, N),
        in_specs=specs(fwd) + specs(bwd),
        out_specs=[pl.BlockSpec((None, C, hw), lambda b, n: (b, fwd(n), 0)),
                   pl.BlockSpec((None, C, hw), lambda b, n: (b, bwd(n), 0))],
        out_shape=[out_sd, out_sd],
        scratch_shapes=[pltpu.VMEM((2, DN_HEADS, DN_DIM, DN_DIM), F32)],
        compiler_params=_cparams(("parallel", "arbitrary")),
        name="deltanet_scan",
    )(qkv, qkv, qkv, bg3, bgT, qkv, qkv, qkv, bg3, bgT)


def _layer_norm(y, g, b):
    mu = jnp.mean(y, axis=-1, keepdims=True)
    var = jnp.mean(jnp.square(y - mu), axis=-1, keepdims=True)
    return (y - mu) * lax.rsqrt(var + LN_EPS) * g + b


def _outproj_kernel(attn_ref, of_ref, ob_ref, z0_ref, z1_ref, x_ref, w_ref, nw_ref, g_ref, b_ref, h_ref, mix_ref):
    aw = ATTN_HEADS * HEAD_DIM
    mix_ref[:, :aw] = attn_ref[...]
    nw = nw_ref[...]
    half = z0_ref.shape[1] // DN_DIM
    for hh in range(DN_HEADS):
        sl = slice(hh * DN_DIM, (hh + 1) * DN_DIM)
        o = of_ref[:, sl] + ob_ref[:, sl]
        z_ref = z0_ref if hh < half else z1_ref
        zc = (hh % half) * DN_DIM
        z = z_ref[:, zc:zc + DN_DIM].astype(F32)
        o = o * lax.rsqrt(jnp.mean(o * o, axis=-1, keepdims=True) + RMS_EPS) * nw
        mix_ref[:, aw + hh * DN_DIM:aw + (hh + 1) * DN_DIM] = (o * jax.nn.silu(z)).astype(BF16)
    y = DEEPNORM_ALPHA * x_ref[...] + _dot(mix_ref[...], w_ref[...])
    h_ref[...] = _layer_norm(y, g_ref[...], b_ref[...])


def _outproj(attn2, of2, ob2, proj2, x2, w_out_b, norm_w, ln_g, ln_b, tm):
    M, D = x2.shape
    dw = DN_HEADS * DN_DIM
    zb = dw // 2
    z0 = COL_DZ // zb
    row = lambda w: pl.BlockSpec((tm, w), lambda i: (i, 0))
    full = lambda shape: pl.BlockSpec(shape, lambda i: (0, 0))
    return pl.pallas_call(
        _outproj_kernel,
        grid=(M // tm,),
        in_specs=[row(ATTN_HEADS * HEAD_DIM), row(dw), row(dw),
                  pl.BlockSpec((tm, zb), lambda i: (i, z0)), pl.BlockSpec((tm, zb), lambda i: (i, z0 + 1)),
                  row(D), full(w_out_b.shape), full((1, DN_DIM)), full((1, D)), full((1, D))],
        out_specs=row(D),
        out_shape=jax.ShapeDtypeStruct((M, D), F32),
        scratch_shapes=[pltpu.VMEM((tm, ATTN_HEADS * HEAD_DIM + dw), BF16)],
        compiler_params=_cparams(("parallel",)),
        name="outproj_ln",
    )(attn2, of2, ob2, proj2, proj2, x2, w_out_b, norm_w.reshape(1, -1), ln_g.reshape(1, -1), ln_b.reshape(1, -1))


TOPK_W = 128
TOPK_HEADS = 4


def _tree_reduce(fn, xs):
    xs = list(xs)
    while len(xs) > 1:
        xs = [fn(xs[i], xs[i + 1]) if i + 1 < len(xs) else xs[i] for i in range(0, len(xs), 2)]
    return xs[0]


def _topk_rounds(ss, labels, val_refs, idx_refs):
    big = jnp.iinfo(jnp.int32).max
    N = range(len(ss))
    tiles = [[s[t:t + 8, :] for t in range(0, s.shape[0], 8)] for s in ss]
    labs = [[lab[t:t + 8, :] for t in range(0, lab.shape[0], 8)] for lab in labels]
    for r in range(PEER_TOPK):
        ms = [jnp.max(_tree_reduce(jnp.maximum, tiles[n]), axis=0, keepdims=True) for n in N]
        firsts = []
        for n in N:
            acc = jnp.full(tiles[n][0].shape, big, I32)
            for tl, lb in zip(reversed(tiles[n]), reversed(labs[n])):
                acc = jnp.where(tl == ms[n], lb, acc)
            firsts.append(acc)
        idxs = [jnp.min(firsts[n], axis=0, keepdims=True) for n in N]
        for n in N:
            val_refs[n][r:r + 1, :] = ms[n]
            idx_refs[n][r:r + 1, :] = idxs[n]
        tiles = [[jnp.where(lb == idxs[n], -jnp.inf, tl) for tl, lb in zip(tiles[n], labs[n])] for n in N]


def _staircase_candidates(sv1, sv2):
    w = sv1.shape[1]
    iota8 = lax.broadcasted_iota(I32, (8, w), 0)
    vals, labs = [], []
    for a in range(8):
        nb = PEER_TOPK // (a + 1)
        for b0 in range(0, nb, 8):
            v = sv1[a:a + 1, :] + sv2[b0:b0 + 8, :]
            if nb - b0 < 8:
                v = jnp.where(iota8 < nb - b0, v, -jnp.inf)
            vals.append(v)
            labs.append(iota8 + (a * PEER_TOPK + b0))
    vals.append(sv1[8:16, :] + sv2[0:1, :])
    labs.append((iota8 + 8) * PEER_TOPK)
    return jnp.concatenate(vals, axis=0), jnp.concatenate(labs, axis=0)


def _pick_rows(sel, table):
    out = jnp.zeros(sel.shape, table.dtype)
    for a in range(PEER_TOPK):
        out = jnp.where(sel == a, table[a:a + 1, :], out)
    return out


def _peer_topk_kernel(h_ref, wq_ref, keys_ref, i_ref, j_ref, g_ref,
                      qs_ref, sv_ref, si_ref, tv_ref, tc_ref, *, tq):
    hd = pl.program_id(1)
    nq = 2 * PEER_HEADS

    @pl.when(hd == 0)
    def _():
        q = _dot(h_ref[...].astype(BF16), wq_ref[...])
        for c in range(nq):
            qs_ref[c] = q[:, c * PEER_HALF:(c + 1) * PEER_HALF].astype(BF16)

    units = [(hh, sl) for hh in range(TOPK_HEADS) for sl in range(tq // TOPK_W)]
    U = range(len(units))
    lanes = [slice(sl * TOPK_W, (sl + 1) * TOPK_W) for _, sl in units]
    rows = [slice(hh * PEER_TOPK, (hh + 1) * PEER_TOPK) for hh, _ in units]
    key_iota = lax.broadcasted_iota(I32, (PEER_NKEYS, TOPK_W), 0)
    probs = [(u, p) for u in U for p in range(2)]
    scores = [_dot_nt(keys_ref[p], qs_ref[2 * (TOPK_HEADS * hd + units[u][0]) + p, lanes[u], :])
              for u, p in probs]
    _topk_rounds(scores, [key_iota] * len(probs),
                 [sv_ref.at[u, p] for u, p in probs], [si_ref.at[u, p] for u, p in probs])
    cands = [_staircase_candidates(sv_ref[u, 0], sv_ref[u, 1]) for u in U]
    _topk_rounds([c[0] for c in cands], [c[1] for c in cands],
                 [tv_ref.at[u] for u in U], [tc_ref.at[u] for u in U])
    shift = int(math.log2(PEER_TOPK))
    for u in U:
        top_s = tv_ref[u]
        top_c = tc_ref[u]
        e = jnp.exp(top_s - top_s[0:1, :])
        g_ref[rows[u], lanes[u]] = e / jnp.sum(e, axis=0, keepdims=True)
        i_ref[rows[u], lanes[u]] = _pick_rows(top_c >> shift, si_ref[u, 0])
        j_ref[rows[u], lanes[u]] = _pick_rows(top_c & (PEER_TOPK - 1), si_ref[u, 1])


def _peer_topk(h2, wq_b, keys_b, tq):
    M, D = h2.shape
    nq = 2 * PEER_HEADS
    out_spec = pl.BlockSpec((TOPK_HEADS * PEER_TOPK, tq), lambda i, hd: (hd, i))
    nu = TOPK_HEADS * (tq // TOPK_W)
    np_ = PEER_HEADS * PEER_TOPK
    return pl.pallas_call(
        functools.partial(_peer_topk_kernel, tq=tq),
        grid=(M // tq, PEER_HEADS // TOPK_HEADS),
        in_specs=[pl.BlockSpec((tq, D), lambda i, hd: (i, 0)),
                  pl.BlockSpec(wq_b.shape, lambda i, hd: (0, 0)),
                  pl.BlockSpec(keys_b.shape, lambda i, hd: (0, 0, 0))],
        out_specs=[out_spec, out_spec, out_spec],
        out_shape=[jax.ShapeDtypeStruct((np_, M), I32), jax.ShapeDtypeStruct((np_, M), I32),
                   jax.ShapeDtypeStruct((np_, M), F32)],
        scratch_shapes=[pltpu.VMEM((nq, tq, PEER_HALF), BF16),
                        pltpu.VMEM((nu, 2, PEER_TOPK, TOPK_W), F32),
                        pltpu.VMEM((nu, 2, PEER_TOPK, TOPK_W), I32),
                        pltpu.VMEM((nu, PEER_TOPK, TOPK_W), F32),
                        pltpu.VMEM((nu, PEER_TOPK, TOPK_W), I32)],
        compiler_params=_cparams(("parallel", "arbitrary")),
        name="peer_topk",
    )(h2, wq_b, keys_b)


def _gate_kernel(i_ref, j_ref, g_ref, o_ref, *, tg):
    sub = lax.broadcasted_iota(I32, (PEER_NKEYS, PEER_HEADS * PEER_TOPK), 0)
    rows = o_ref.shape[2]

    def body(t, carry):
        irow = i_ref[pl.ds(t, 1), :]
        jrow = j_ref[pl.ds(t, 1), :]
        grow = g_ref[pl.ds(t, 1), :]
        at = jnp.where(sub == irow, grow, 0.0).astype(BF16)
        bt = jnp.where(sub == jrow, 1.0, 0.0).astype(BF16)
        gt = _dot_nt(at, bt)
        for blk in range(o_ref.shape[0]):
            o_ref[blk, t] = gt[blk * rows:(blk + 1) * rows, :]
        return carry

    lax.fori_loop(0, tg, body, 0, unroll=True)


def _gate_matrix(i_n, j_n, g_n, tg, tt, rows):
    M, P = i_n.shape
    nblk = PEER_NKEYS // rows
    per_tile = tt // tg
    spec = pl.BlockSpec((tg, P), lambda i: (i, 0))
    return pl.pallas_call(
        functools.partial(_gate_kernel, tg=tg),
        grid=(M // tg,),
        in_specs=[spec, spec, spec],
        out_specs=pl.BlockSpec((None, nblk, tg, rows, PEER_NKEYS),
                               lambda i: (i // per_tile, 0, i % per_tile, 0, 0)),
        out_shape=jax.ShapeDtypeStruct((M // tt, nblk, tt, rows, PEER_NKEYS), F32),
        compiler_params=_cparams(("parallel",)),
        name="peer_gates",
    )(i_n, j_n, g_n)


DENSE_RING = 3


def _peer_dense_kernel(h_ref, u_hbm, v_hbm, gm_ref, g_ref, b_ref, o_ref, hb_ref, p_ref, ubuf, vbuf, sem, *, eb):
    j = pl.program_id(1)
    nblk = pl.num_programs(1)
    step = pl.program_id(0) * nblk + j
    total = pl.num_programs(0) * nblk

    def copies(st, slot):
        rows = pl.ds(pl.multiple_of((st % nblk) * eb, eb), eb)
        return (pltpu.make_async_copy(u_hbm.at[rows, :], ubuf.at[slot], sem.at[0, slot]),
                pltpu.make_async_copy(v_hbm.at[rows, :], vbuf.at[slot], sem.at[1, slot]))

    @pl.when(step == 0)
    def _():
        for st in range(DENSE_RING - 1):
            for c in copies(st, st):
                c.start()

    ahead = step + (DENSE_RING - 1)

    @pl.when(ahead < total)
    def _():
        for c in copies(ahead, ahead % DENSE_RING):
            c.start()

    slot = step % DENSE_RING
    for c in copies(step, slot):
        c.wait()

    @pl.when(j == 0)
    def _():
        hb_ref[...] = h_ref[...].astype(BF16)
        o_ref[...] = jnp.zeros_like(o_ref)

    s = _dot_nt(hb_ref[...], ubuf[slot])
    act = 0.5 * s * (1.0 + lax.erf(s * (2.0 ** -0.5)))
    tt = s.shape[0]
    rows = gm_ref.shape[0] // tt
    for r in range(rows):
        sl = slice(r * PEER_NKEYS, (r + 1) * PEER_NKEYS)
        p_ref[:, sl] = (act[:, sl] * gm_ref[pl.ds(r, tt, stride=rows), :]).astype(BF16)
    o_ref[...] += _dot(p_ref[...], vbuf[slot])

    @pl.when(j == pl.num_programs(1) - 1)
    def _():
        y = DEEPNORM_ALPHA * h_ref[...] + o_ref[...]
        o_ref[...] = _layer_norm(y, g_ref[...], b_ref[...])


def _peer_dense(h2, u_b, v_b, gmat, ln_g, ln_b, tt, eb):
    M, D = h2.shape
    E = u_b.shape[0]
    rows = eb // PEER_NKEYS
    return pl.pallas_call(
        functools.partial(_peer_dense_kernel, eb=eb),
        grid=(M // tt, E // eb),
        in_specs=[pl.BlockSpec((tt, D), lambda i, j: (i, 0)),
                  pl.BlockSpec(memory_space=pl.ANY),
                  pl.BlockSpec(memory_space=pl.ANY),
                  pl.BlockSpec((None, None, tt * rows, PEER_NKEYS), lambda i, j: (i, j, 0, 0)),
                  pl.BlockSpec((1, D), lambda i, j: (0, 0)),
                  pl.BlockSpec((1, D), lambda i, j: (0, 0))],
        out_specs=pl.BlockSpec((tt, D), lambda i, j: (i, 0)),
        out_shape=jax.ShapeDtypeStruct((M, D), F32),
        scratch_shapes=[pltpu.VMEM((tt, D), BF16), pltpu.VMEM((tt, eb), BF16),
                        pltpu.VMEM((DENSE_RING, eb, D), BF16), pltpu.VMEM((DENSE_RING, eb, D), BF16),
                        pltpu.SemaphoreType.DMA((2, DENSE_RING))],
        compiler_params=_cparams(("arbitrary", "arbitrary")),
        name="peer_dense",
    )(h2, u_b, v_b, gmat, ln_g.reshape(1, -1), ln_b.reshape(1, -1))


def _tile(n, pref):
    t = min(n, pref)
    assert n % t == 0
    return t


def _layer(x, w_in, conv_w, a_log, dt_bias, dn_norm_w, attn_sink, rel_bias, w_out,
           ln1_g, ln1_b, peer_wq, peer_keys, peer_u, peer_v, ln2_g, ln2_b):
    B, S, D = x.shape
    T = B * S
    x2 = x.reshape(T, D)

    proj, bg, bgT = _inproj(x2, w_in[:, :COL_TAIL].astype(BF16), w_in[:, COL_TAIL:], a_log, dt_bias,
                            _tile(T, 1024), COL_TAIL // 2)
    proj3 = proj.reshape(B, S, -1)

    sink_b = jnp.broadcast_to(attn_sink.astype(F32).reshape(ATTN_KV_HEADS, ATTN_GROUP, 1),
                              (ATTN_KV_HEADS, ATTN_GROUP, 128))
    attn = _attention(proj3, _attn_bias_table(rel_bias), sink_b)

    qkv = _conv(proj3, conv_w, _tile(S, 1024))
    o_f, o_b = _deltanet(qkv, bg.reshape(B, S, 128), bgT)

    h = _outproj(attn.reshape(T, -1), o_f.reshape(T, -1), o_b.reshape(T, -1), proj, x2,
                 w_out.astype(BF16), dn_norm_w, ln1_g, ln1_b, _tile(T, 512))

    i_t, j_t, g_t = _peer_topk(h, peer_wq.astype(BF16), peer_keys.astype(BF16), _tile(T, 256))
    tt, eb = _tile(T, 1024), 512
    rows = eb // PEER_NKEYS
    gmat = _gate_matrix(i_t.T, j_t.T, g_t.T, _tile(T, 128), tt, rows)
    gmat = gmat.reshape(T // tt, PEER_NKEYS // rows, tt * rows, PEER_NKEYS)
    y = _peer_dense(h, peer_u.astype(BF16), peer_v.astype(BF16), gmat, ln2_g, ln2_b, tt, eb)
    return y.reshape(B, S, D)


def kernel(x, w_in, conv_w, a_log, dt_bias, dn_norm_w, attn_sink, rel_bias, w_out, ln1_g, ln1_b,
           peer_wq, peer_keys, peer_u, peer_v, ln2_g, ln2_b):
    for l in range(DEPTH):
        x = _layer(x, w_in[l], conv_w[l], a_log[l], dt_bias[l], dn_norm_w[l], attn_sink[l], rel_bias,
                   w_out[l], ln1_g[l], ln1_b[l], peer_wq[l], peer_keys[l], peer_u[l], peer_v[l],
                   ln2_g[l], ln2_b[l])
    return x
```
